```python
import jax, jax.numpy as jnp
from jax import lax
import numpy as np

D_MODEL = 1024
BATCH = 32
SEQ = 2048
DEPTH = 2
DEC_BATCH = 2
DEC_SEQ = 8192
PAST_LEN = 128

GRID_W = 64
ROPE_THETA = 10000.0
Q_BLOCK = 128
EPS = 1e-6

H_A = 8
KV_A = 2
HD_A = 64
H_B = 8
Q_LORA = 384
KV_LORA = 256
NOPE_B = 64
ROPE_B = 32
V_B = 64
QK_B = NOPE_B + ROPE_B

D_FF = 2816
CONV_W = 3

W_QA = H_A * HD_A
W_KA = KV_A * HD_A
W_OB = H_B * V_B
N_IN = W_QA + 2 * W_KA + Q_LORA + KV_LORA + ROPE_B + 2 * D_MODEL
SPLIT_POINTS = tuple(int(v) for v in np.cumsum([W_QA, W_KA, W_KA, Q_LORA, KV_LORA, ROPE_B]))

kernel_name = "hybrid_gqa_mla_gated_encoder"


def _rmsnorm(x, g):
    xf = x.astype(jnp.float32)
    y = xf * lax.rsqrt(jnp.mean(xf * xf, axis=-1, keepdims=True) + EPS)
    return (y * g.astype(jnp.float32)).astype(x.dtype)


def _axial_angles(seq_len, rot_dim):
    rows = seq_len // GRID_W
    row_ids = jnp.repeat(jnp.arange(rows, dtype=jnp.float32), GRID_W)
    col_ids = jnp.tile(jnp.arange(GRID_W, dtype=jnp.float32), rows)
    half = rot_dim // 2
    inv = ROPE_THETA ** (-jnp.arange(0, half, 2, dtype=jnp.float32) / half)
    ang = jnp.concatenate([row_ids[:, None] * inv, col_ids[:, None] * inv], axis=-1)
    return jnp.cos(ang), jnp.sin(ang)


def _apply_rope(x, cos, sin):
    xf = x.astype(jnp.float32).reshape(x.shape[:-1] + (x.shape[-1] // 2, 2))
    x0, x1 = xf[..., 0], xf[..., 1]
    c = cos[None, :, None, :]
    s = sin[None, :, None, :]
    out = jnp.stack([x0 * c - x1 * s, x0 * s + x1 * c], axis=-1).reshape(x.shape)
    return out.astype(x.dtype)


def _block_attention(q, k, v, scale):
    B, S, H, dk = q.shape
    Hkv = k.shape[2]
    dv = v.shape[-1]
    G = H // Hkv
    nb = S // Q_BLOCK
    qb = q.reshape(B, nb, Q_BLOCK, Hkv, G, dk).transpose(1, 0, 2, 3, 4, 5)
    kf = k.astype(jnp.float32)
    vf = v.astype(jnp.float32)

    def one_block(qi):
        s = jnp.einsum('bqkgd,bskd->bkgqs', qi.astype(jnp.float32), kf) * scale
        p = jax.nn.softmax(s, axis=-1)
        return jnp.einsum('bkgqs,bskd->bqkgd', p, vf).astype(v.dtype)

    out = lax.map(one_block, qb)
    return out.transpose(1, 0, 2, 3, 4, 5).reshape(B, S, H, dv)


def _dwconv(x, w, b):
    C = x.shape[-1]
    y = lax.conv_general_dilated(
        x, w[:, None, :].astype(x.dtype), window_strides=(1,),
        padding=[((CONV_W - 1) // 2, (CONV_W - 1) // 2)],
        dimension_numbers=('NWC', 'WIO', 'NWC'), feature_group_count=C)
    return y + b.astype(x.dtype)


def _layer(x, cos_a, sin_a, cos_b, sin_b,
           g_mix_pre, w_in, g_qa, g_ka, g_cq, w_uq, g_ckv, w_ukv,
           w_oa, w_ob, b_gates, w_out, g_mix_post,
           g_ffn_pre, w_up, conv_w, conv_b, w_down, g_ffn_post):
    B, S, _ = x.shape
    h = _rmsnorm(x, g_mix_pre)
    proj = h @ w_in
    qa, ka, va, cq, ckv, kr, gates = jnp.split(proj, SPLIT_POINTS, axis=-1)

    qa = _apply_rope(_rmsnorm(qa.reshape(B, S, H_A, HD_A), g_qa), cos_a, sin_a)
    ka = _apply_rope(_rmsnorm(ka.reshape(B, S, KV_A, HD_A), g_ka), cos_a, sin_a)
    va = va.reshape(B, S, KV_A, HD_A)
    ya = _block_attention(qa, ka, va, HD_A ** -0.5).reshape(B, S, W_QA) @ w_oa

    qb = (_rmsnorm(cq, g_cq) @ w_uq).reshape(B, S, H_B, QK_B)
    qb = jnp.concatenate([qb[..., :NOPE_B], _apply_rope(qb[..., NOPE_B:], cos_b, sin_b)], axis=-1)
    kvb = (_rmsnorm(ckv, g_ckv) @ w_ukv).reshape(B, S, H_B, NOPE_B + V_B)
    kr = _apply_rope(kr.reshape(B, S, 1, ROPE_B), cos_b, sin_b)
    kb = jnp.concatenate([kvb[..., :NOPE_B], jnp.broadcast_to(kr, (B, S, H_B, ROPE_B))], axis=-1)
    vb = kvb[..., NOPE_B:]
    yb = _block_attention(qb, kb, vb, QK_B ** -0.5).reshape(B, S, W_OB) @ w_ob

    ga, gb = jnp.split(jax.nn.sigmoid(gates + b_gates), 2, axis=-1)
    mixed = (ga * ya + gb * yb) @ w_out
    x = x + _rmsnorm(mixed, g_mix_post)

    h = _rmsnorm(x, g_ffn_pre)
    u = _dwconv(h @ w_up, conv_w, conv_b)
    gate, val = jnp.split(u, 2, axis=-1)
    f = (jax.nn.gelu(gate, approximate=True) * val) @ w_down
    return x + _rmsnorm(f, g_ffn_post)


def _trunk(x, g_mix_pre, w_in, g_qa, g_ka, g_cq, w_uq, g_ckv, w_ukv,
           w_oa, w_ob, b_gates, w_out, g_mix_post,
           g_ffn_pre, w_up, conv_w, conv_b, w_down, g_ffn_post):
    S = x.shape[1]
    cos_a, sin_a = _axial_angles(S, HD_A)
    cos_b, sin_b = _axial_angles(S, ROPE_B)
    for l in range(DEPTH):
        x = _layer(x, cos_a, sin_a, cos_b, sin_b,
                   g_mix_pre[l], w_in[l], g_qa[l], g_ka[l], g_cq[l], w_uq[l], g_ckv[l], w_ukv[l],
                   w_oa[l], w_ob[l], b_gates[l], w_out[l], g_mix_post[l],
                   g_ffn_pre[l], w_up[l], conv_w[l], conv_b[l], w_down[l], g_ffn_post[l])
    return x


def setup_inputs(seed: int = 0) -> dict:
    key = jax.random.key(seed)
    ks = jax.random.split(key, 24)
    f32 = jnp.float32

    def w(k, shape, fan_in):
        return jax.random.normal(k, shape, f32) * (fan_in ** -0.5)

    def gain(k, shape):
        return 1.0 + 0.05 * jax.random.normal(k, shape, f32)

    L = DEPTH
    return {
        "x_prompt": jax.random.normal(ks[0], (BATCH, SEQ, D_MODEL), f32),
        "x_sample": jax.random.normal(ks[1], (DEC_BATCH, DEC_SEQ, D_MODEL), f32),
        "g_mix_pre": gain(ks[2], (L, D_MODEL)),
        "w_in": w(ks[3], (L, D_MODEL, N_IN), D_MODEL),
        "g_qa": gain(ks[4], (L, HD_A)),
        "g_ka": gain(ks[5], (L, HD_A)),
        "g_cq": gain(ks[6], (L, Q_LORA)),
        "w_uq": w(ks[7], (L, Q_LORA, H_B * QK_B), Q_LORA),
        "g_ckv": gain(ks[8], (L, KV_LORA)),
        "w_ukv": w(ks[9], (L, KV_LORA, H_B * (NOPE_B + V_B)), KV_LORA),
        "w_oa": w(ks[10], (L, W_QA, D_MODEL), W_QA),
        "w_ob": w(ks[11], (L, W_OB, D_MODEL), W_OB),
        "b_gates": 0.1 * jax.random.normal(ks[12], (L, 2 * D_MODEL), f32),
        "w_out": w(ks[13], (L, D_MODEL, D_MODEL), D_MODEL),
        "g_mix_post": gain(ks[14], (L, D_MODEL)),
        "g_ffn_pre": gain(ks[15], (L, D_MODEL)),
        "w_up": w(ks[16], (L, D_MODEL, 2 * D_FF), D_MODEL),
        "conv_w": w(ks[17], (L, CONV_W, 2 * D_FF), CONV_W),
        "conv_b": 0.01 * jax.random.normal(ks[18], (L, 2 * D_FF), f32),
        "w_down": w(ks[19], (L, D_FF, D_MODEL), D_FF),
        "g_ffn_post": gain(ks[20], (L, D_MODEL)),
    }


def reference(x_prompt, x_sample, g_mix_pre, w_in, g_qa, g_ka, g_cq, w_uq, g_ckv, w_ukv,
              w_oa, w_ob, b_gates, w_out, g_mix_post,
              g_ffn_pre, w_up, conv_w, conv_b, w_down, g_ffn_post):
    y_prompt = _trunk(x_prompt, g_mix_pre, w_in, g_qa, g_ka, g_cq, w_uq, g_ckv, w_ukv,
                      w_oa, w_ob, b_gates, w_out, g_mix_post,
                      g_ffn_pre, w_up, conv_w, conv_b, w_down, g_ffn_post)
    y_sample = _trunk(x_sample, g_mix_pre, w_in, g_qa, g_ka, g_cq, w_uq, g_ckv, w_ukv,
                      w_oa, w_ob, b_gates, w_out, g_mix_post,
                      g_ffn_pre, w_up, conv_w, conv_b, w_down, g_ffn_post)
    return (y_prompt, y_sample)
```

```python
import functools

import jax
import jax.numpy as jnp
import numpy as np
from jax import lax
from jax.experimental import pallas as pl
from jax.experimental.pallas import tpu as pltpu

D_MODEL = 1024
GRID_W = 64
ROPE_THETA = 10000.0
EPS = 1e-6

H_A = 8
KV_A = 2
HD_A = 64
H_B = 8
Q_LORA = 384
KV_LORA = 256
NOPE_B = 64
ROPE_B = 32
V_B = 64
QK_B = NOPE_B + ROPE_B
D_FF = 2816
CONV_W = 3

W_QA = H_A * HD_A
W_KA = KV_A * HD_A
W_OB = H_B * V_B

LANE = 128
HALF = LANE // 2
HALO = 8
VMEM_LIMIT = 56 * 1024 * 1024

BF16 = jnp.bfloat16
F32 = jnp.float32


def _head_cols_rot(base, dim):
    half = dim // 2
    cols = np.full((LANE,), -1, np.int64)
    cols[:half] = base + np.arange(0, dim, 2)
    cols[HALF:HALF + half] = base + np.arange(1, dim, 2)
    return cols


def _head_cols_plain(base, dim):
    cols = np.full((LANE,), -1, np.int64)
    cols[:dim] = base + np.arange(dim)
    return cols


def _mla_cols(nope_base, rope_base):
    cols = np.full((LANE,), -1, np.int64)
    hn = NOPE_B // 2
    hr = ROPE_B // 2
    if nope_base is not None:
        cols[:hn] = nope_base + np.arange(hn)
        cols[HALF:HALF + hn] = nope_base + hn + np.arange(hn)
    if rope_base is not None:
        cols[hn:hn + hr] = rope_base + np.arange(0, ROPE_B, 2)
        cols[HALF + hn:HALF + hn + hr] = rope_base + np.arange(1, ROPE_B, 2)
    return cols


def _in_proj_cols():
    o_ka = W_QA
    o_va = o_ka + W_KA
    o_cq = o_va + W_KA
    o_ckv = o_cq + Q_LORA
    o_kr = o_ckv + KV_LORA
    blocks = [_head_cols_rot(h * HD_A, HD_A) for h in range(H_A)]
    blocks += [_head_cols_rot(o_ka + j * HD_A, HD_A) for j in range(KV_A)]
    blocks += [_head_cols_plain(o_va + j * HD_A, HD_A) for j in range(KV_A)]
    blocks += [o_cq + np.arange(Q_LORA), o_ckv + np.arange(KV_LORA)]
    blocks += [_mla_cols(None, o_kr)]
    return np.concatenate(blocks), o_kr + ROPE_B


def _uq_cols():
    return np.concatenate([_mla_cols(h * QK_B, h * QK_B + NOPE_B) for h in range(H_B)])


def _ukv_cols():
    stride = NOPE_B + V_B
    k = [_mla_cols(h * stride, None) for h in range(H_B)]
    v = [_head_cols_plain(h * stride + NOPE_B, V_B) for h in range(H_B)]
    return np.concatenate(k + v)


def _gather_cols(w, cols):
    taken = jnp.take(w, jnp.asarray(np.maximum(cols, 0), jnp.int32), axis=-1)
    return jnp.where(jnp.asarray(cols >= 0), taken, 0.0)


def _rot_tables(seq_len):
    rows = seq_len // GRID_W
    row_ids = jnp.repeat(jnp.arange(rows, dtype=F32), GRID_W)
    col_ids = jnp.tile(jnp.arange(GRID_W, dtype=F32), rows)

    def angles(rot_dim):
        half = rot_dim // 2
        inv = ROPE_THETA ** (-jnp.arange(0, half, 2, dtype=F32) / half)
        ang = jnp.concatenate([row_ids[:, None] * inv, col_ids[:, None] * inv], axis=-1)
        return jnp.cos(ang), jnp.sin(ang)

    def place(cos, sin, lo):
        n = cos.shape[1]
        c = jnp.ones((seq_len, LANE), F32)
        s = jnp.zeros((seq_len, LANE), F32)
        c = c.at[:, lo:lo + n].set(cos).at[:, HALF + lo:HALF + lo + n].set(cos)
        s = s.at[:, lo:lo + n].set(-sin).at[:, HALF + lo:HALF + lo + n].set(sin)
        return c, s

    ca, sa = place(*angles(HD_A), 0)
    cb, sb = place(*angles(ROPE_B), NOPE_B // 2)
    return jnp.stack([ca, sa, cb, sb])


def _rms(x, g):
    return x * lax.rsqrt(jnp.mean(x * x, axis=-1, keepdims=True) + EPS) * g


def _rope(y, cos, sin):
    return y * cos + pltpu.roll(y, HALF, axis=1) * sin


def _dot(a, b):
    return jnp.dot(a, b, preferred_element_type=F32)


def _const_spec(shape):
    zeros = (0,) * len(shape)
    return pl.BlockSpec(shape, lambda *_: zeros)


def _params(n_axes, last_arbitrary=False):
    sem = ["parallel"] * n_axes
    if last_arbitrary:
        sem[-1] = "arbitrary"
    return pltpu.CompilerParams(dimension_semantics=tuple(sem), vmem_limit_bytes=VMEM_LIMIT)


def _in_proj_kernel(x_ref, tab_ref, g_pre_ref, w1_ref, gqa_ref, gka_ref, gcq_ref, gckv_ref,
                    wuq_ref, wukv_ref,
                    qa_ref, ka_ref, va_ref, qb_ref, kb_ref, vb_ref):
    ca, sa, cb, sb = tab_ref[0], tab_ref[1], tab_ref[2], tab_ref[3]
    lane = lax.broadcasted_iota(jnp.int32, (1, LANE), 1)
    low = lane < HALF

    h = _rms(x_ref[0], g_pre_ref[...]).astype(BF16)
    proj = _dot(h, w1_ref[...])

    def head_norm_rope(blk, g):
        ms = jnp.sum(blk * blk, axis=-1, keepdims=True) * (1.0 / HD_A)
        return _rope(blk * lax.rsqrt(ms + EPS) * g, ca, sa)

    off = 0
    for hh in range(H_A):
        blk = proj[:, off:off + LANE]
        qa_ref[0, :, hh * LANE:(hh + 1) * LANE] = head_norm_rope(blk, gqa_ref[...]).astype(BF16)
        off += LANE
    for j in range(KV_A):
        blk = proj[:, off:off + LANE]
        ka_ref[0, :, j * LANE:(j + 1) * LANE] = head_norm_rope(blk, gka_ref[...]).astype(BF16)
        off += LANE
    for j in range(KV_A):
        blk = proj[:, off:off + LANE]
        va_ref[0, :, j * LANE:(j + 1) * LANE] = jnp.where(low, blk, 1.0).astype(BF16)
        off += LANE

    cq = _rms(proj[:, off:off + Q_LORA], gcq_ref[...]).astype(BF16)
    off += Q_LORA
    ckv = _rms(proj[:, off:off + KV_LORA], gckv_ref[...]).astype(BF16)
    off += KV_LORA
    kr = _rope(proj[:, off:off + LANE], cb, sb)

    qb = _dot(cq, wuq_ref[...])
    kvb = _dot(ckv, wukv_ref[...])
    scale_b = QK_B ** -0.5
    for hh in range(H_B):
        sl = slice(hh * LANE, (hh + 1) * LANE)
        qb_ref[0, :, sl] = (_rope(qb[:, sl], cb, sb) * scale_b).astype(BF16)
        kb_ref[0, :, sl] = (kvb[:, sl] + kr).astype(BF16)
        vblk = kvb[:, H_B * LANE + hh * LANE:H_B * LANE + (hh + 1) * LANE]
        vb_ref[0, :, sl] = jnp.where(low, vblk, 1.0).astype(BF16)


def _in_proj(x, tabs, g_pre, w1, gqa, gka, gcq, gckv, wuq, wukv, tm):
    B, S, _ = x.shape
    n1 = w1.shape[1]
    tok = lambda w: pl.BlockSpec((1, tm, w), lambda b, i: (b, i, 0))
    out = lambda w: jax.ShapeDtypeStruct((B, S, w), BF16)
    return pl.pallas_call(
        _in_proj_kernel,
        grid=(B, S // tm),
        in_specs=[
            tok(D_MODEL),
            pl.BlockSpec((4, tm, LANE), lambda b, i: (0, i, 0)),
            _const_spec((1, D_MODEL)),
            _const_spec((D_MODEL, n1)),
            _const_spec((1, LANE)), _const_spec((1, LANE)),
            _const_spec((1, Q_LORA)), _const_spec((1, KV_LORA)),
            _const_spec((Q_LORA, H_B * LANE)),
            _const_spec((KV_LORA, 2 * H_B * LANE)),
        ],
        out_specs=[tok(H_A * LANE), tok(KV_A * LANE), tok(KV_A * LANE),
                   tok(H_B * LANE), tok(H_B * LANE), tok(H_B * LANE)],
        out_shape=[out(H_A * LANE), out(KV_A * LANE), out(KV_A * LANE),
                   out(H_B * LANE), out(H_B * LANE), out(H_B * LANE)],
        compiler_params=_params(2),
        name="in_proj",
    )(x, tabs, g_pre, w1, gqa, gka, gcq, gckv, wuq, wukv)


def _attn_kernel(q_ref, k_ref, v_ref, o_ref, *scratch, n_kv, group, n_chunks):
    tq = q_ref.shape[1]
    kc = pl.program_id(2)
    lane = lax.broadcasted_iota(jnp.int32, (1, LANE), 1)
    low = lane < HALF

    def finish(acc_heads):
        normed = [a * (1.0 / a[:, HALF:HALF + 1]) for a in acc_heads]
        for c in range(len(normed) // 2):
            pair = jnp.where(low, normed[2 * c], pltpu.roll(normed[2 * c + 1], HALF, axis=1))
            o_ref[0, :, c * LANE:(c + 1) * LANE] = pair.astype(o_ref.dtype)

    if n_chunks > 1:
        m_ref, acc_ref = scratch

        @pl.when(kc == 0)
        def _():
            m_ref[...] = jnp.full(m_ref.shape, -jnp.inf, F32)
            acc_ref[...] = jnp.zeros(acc_ref.shape, F32)

    heads = []
    for j in range(n_kv):
        qs = jnp.concatenate(
            [q_ref[0, :, (j * group + g) * LANE:(j * group + g + 1) * LANE] for g in range(group)],
            axis=0)
        kj = k_ref[0, :, j * LANE:(j + 1) * LANE]
        vj = v_ref[0, :, j * LANE:(j + 1) * LANE]
        s = lax.dot_general(qs, kj, (((1,), (1,)), ((), ())), preferred_element_type=F32)
        m_cur = jnp.max(s, axis=-1, keepdims=True)
        if n_chunks == 1:
            p = jnp.exp(s - m_cur).astype(BF16)
            acc = _dot(p, vj)
        else:
            m_prev = m_ref[j]
            m_new = jnp.maximum(m_prev, m_cur)
            alpha = jnp.exp(m_prev - m_new)
            p = jnp.exp(s - m_new[:, :1]).astype(BF16)
            acc = alpha * acc_ref[j] + _dot(p, vj)
            m_ref[j] = m_new
            acc_ref[j] = acc
        heads.append(acc)

    def emit():
        per_head = []
        for j in range(n_kv):
            acc = heads[j] if n_chunks == 1 else acc_ref[j]
            per_head += [acc[g * tq:(g + 1) * tq] for g in range(group)]
        finish(per_head)

    if n_chunks == 1:
        emit()
    else:
        pl.when(kc == n_chunks - 1)(emit)


def _attention(q, k, v, n_kv, group, tq, tk):
    B, S, _ = q.shape
    n_heads = n_kv * group
    n_chunks = S // tk
    rows = group * tq
    scratch = []
    if n_chunks > 1:
        scratch = [pltpu.VMEM((n_kv, rows, LANE), F32), pltpu.VMEM((n_kv, rows, LANE), F32)]
    kernel = functools.partial(_attn_kernel, n_kv=n_kv, group=group, n_chunks=n_chunks)
    return pl.pallas_call(
        kernel,
        grid=(B, S // tq, n_chunks),
        in_specs=[
            pl.BlockSpec((1, tq, n_heads * LANE), lambda b, i, c: (b, i, 0)),
            pl.BlockSpec((1, tk, n_kv * LANE), lambda b, i, c: (b, c, 0)),
            pl.BlockSpec((1, tk, n_kv * LANE), lambda b, i, c: (b, c, 0)),
        ],
        out_specs=pl.BlockSpec((1, tq, n_heads * HALF), lambda b, i, c: (b, i, 0)),
        out_shape=jax.ShapeDtypeStruct((B, S, n_heads * HALF), BF16),
        scratch_shapes=scratch,
        compiler_params=_params(3, last_arbitrary=True),
        name=f"attention_g{group}",
    )(q, k, v)


def _merge_kernel(x_ref, ya_ref, yb_ref, g_pre_ref, wg_ref, bg_ref, woa_ref, wob_ref, wout_ref,
                  g_post_ref, o_ref):
    x = x_ref[0]
    h = _rms(x, g_pre_ref[...]).astype(BF16)
    gates = jax.nn.sigmoid(_dot(h, wg_ref[...]) + bg_ref[...])
    pa = _dot(ya_ref[0], woa_ref[...])
    pb = _dot(yb_ref[0], wob_ref[...])
    merged = (gates[:, :D_MODEL] * pa + gates[:, D_MODEL:] * pb).astype(BF16)
    mixed = _dot(merged, wout_ref[...])
    o_ref[0] = x + _rms(mixed, g_post_ref[...])


def _merge(x, ya, yb, g_pre, wg, bg, woa, wob, wout, g_post, tm):
    B, S, _ = x.shape
    tok = lambda w: pl.BlockSpec((1, tm, w), lambda b, i: (b, i, 0))
    return pl.pallas_call(
        _merge_kernel,
        grid=(B, S // tm),
        in_specs=[
            tok(D_MODEL), tok(W_QA), tok(W_OB),
            _const_spec((1, D_MODEL)),
            _const_spec((D_MODEL, 2 * D_MODEL)), _const_spec((1, 2 * D_MODEL)),
            _const_spec((W_QA, D_MODEL)), _const_spec((W_OB, D_MODEL)),
            _const_spec((D_MODEL, D_MODEL)), _const_spec((1, D_MODEL)),
        ],
        out_specs=tok(D_MODEL),
        out_shape=jax.ShapeDtypeStruct((B, S, D_MODEL), F32),
        compiler_params=_params(2),
        name="gated_merge",
    )(x, ya, yb, g_pre, wg, bg, woa, wob, wout, g_post)


def _ffn_kernel(x_ref, xp_ref, xn_ref, g_pre_ref, wup_ref, cw_ref, cb_ref, wdown_ref, g_post_ref,
                o_ref, f_ref, *, n_chunks):
    tm = x_ref.shape[1]
    i = pl.program_id(1)
    x = x_ref[0]
    xp = jnp.where(i > 0, xp_ref[0, 0], 0.0)
    xn = jnp.where(i < pl.num_programs(1) - 1, xn_ref[0, 0], 0.0)
    xa = jnp.concatenate([xp, x, xn], axis=0)
    h = _rms(xa, g_pre_ref[...]).astype(BF16)
    rows = tm + 2 * HALO
    width = D_FF // n_chunks

    def conv(col0):
        u = _dot(h, wup_ref[:, col0:col0 + width])
        um = pltpu.roll(u, 1, axis=0)[HALO:HALO + tm]
        un = pltpu.roll(u, rows - 1, axis=0)[HALO:HALO + tm]
        cw = cw_ref[:, col0:col0 + width]
        return (um * cw[0:1] + u[HALO:HALO + tm] * cw[1:2] + un * cw[2:3]
                + cb_ref[:, col0:col0 + width])

    for c in range(n_chunks):
        gate = conv(c * width)
        val = conv(D_FF + c * width)
        f_ref[:, c * width:(c + 1) * width] = (jax.nn.gelu(gate, approximate=True) * val).astype(BF16)

    out = _dot(f_ref[...], wdown_ref[...])
    o_ref[0] = x + _rms(out, g_post_ref[...])


def _ffn(x, g_pre, wup, cw, cb, wdown, g_post, tm, n_chunks):
    B, S, _ = x.shape
    x_rows = x.reshape(B, S // HALO, HALO, D_MODEL)
    per = tm // HALO
    last = S // HALO - 1
    tok = pl.BlockSpec((1, tm, D_MODEL), lambda b, i: (b, i, 0))
    kernel = functools.partial(_ffn_kernel, n_chunks=n_chunks)
    return pl.pallas_call(
        kernel,
        grid=(B, S // tm),
        in_specs=[
            tok,
            pl.BlockSpec((1, 1, HALO, D_MODEL), lambda b, i: (b, jnp.maximum(i * per - 1, 0), 0, 0)),
            pl.BlockSpec((1, 1, HALO, D_MODEL), lambda b, i: (b, jnp.minimum((i + 1) * per, last), 0, 0)),
            _const_spec((1, D_MODEL)),
            _const_spec((D_MODEL, 2 * D_FF)),
            _const_spec((CONV_W, 2 * D_FF)), _const_spec((1, 2 * D_FF)),
            _const_spec((D_FF, D_MODEL)), _const_spec((1, D_MODEL)),
        ],
        out_specs=tok,
        out_shape=jax.ShapeDtypeStruct((B, S, D_MODEL), F32),
        scratch_shapes=[pltpu.VMEM((tm, D_FF), BF16)],
        compiler_params=_params(2),
        name="channel_mixer",
    )(x, x_rows, x_rows, g_pre, wup, cw, cb, wdown, g_post)


def _prepare_weights(g_mix_pre, w_in, g_qa, g_ka, g_cq, w_uq, g_ckv, w_ukv, w_oa, w_ob, b_gates,
                     w_out, g_mix_post, g_ffn_pre, w_up, conv_w, conv_b, w_down, g_ffn_post):
    cols1, gate0 = _in_proj_cols()
    rot = _head_cols_rot(0, HD_A)
    row = lambda a: a[:, None, :]
    return dict(
        g_pre=row(g_mix_pre),
        w1=_gather_cols(w_in, cols1).astype(BF16),
        wg=w_in[:, :, gate0:].astype(BF16),
        gqa=row(_gather_cols(g_qa, rot)) * (HD_A ** -0.5),
        gka=row(_gather_cols(g_ka, rot)),
        gcq=row(g_cq), gckv=row(g_ckv),
        wuq=_gather_cols(w_uq, _uq_cols()).astype(BF16),
        wukv=_gather_cols(w_ukv, _ukv_cols()).astype(BF16),
        woa=w_oa.astype(BF16), wob=w_ob.astype(BF16), bg=row(b_gates),
        wout=w_out.astype(BF16), g_post=row(g_mix_post),
        g_ffn_pre=row(g_ffn_pre), wup=w_up.astype(BF16), cw=conv_w, cb=row(conv_b),
        wdown=w_down.astype(BF16), g_ffn_post=row(g_ffn_post),
    )


def _tiles(seq_len):
    return dict(tm=512, tq_a=256, tq_b=256, tk=min(seq_len, 2048))


def _trunk(x, w):
    S = x.shape[1]
    t = _tiles(S)
    tabs = _rot_tables(S)
    depth = w["w1"].shape[0]
    for l in range(depth):
        p = {k: v[l] for k, v in w.items()}
        qa, ka, va, qb, kb, vb = _in_proj(x, tabs, p["g_pre"], p["w1"], p["gqa"], p["gka"],
                                          p["gcq"], p["gckv"], p["wuq"], p["wukv"], t["tm"])
        ya = _attention(qa, ka, va, KV_A, H_A // KV_A, t["tq_a"], t["tk"])
        yb = _attention(qb, kb, vb, H_B, 1, t["tq_b"], t["tk"])
        x = _merge(x, ya, yb, p["g_pre"], p["wg"], p["bg"], p["woa"], p["wob"], p["wout"],
                   p["g_post"], t["tm"])
        x = _ffn(x, p["g_ffn_pre"], p["wup"], p["cw"], p["cb"], p["wdown"], p["g_ffn_post"],
                 t["tm"], n_chunks=2)
    return x


def kernel(x_prompt, x_sample, g_mix_pre, w_in, g_qa, g_ka, g_cq, w_uq, g_ckv, w_ukv, w_oa, w_ob,
           b_gates, w_out, g_mix_post, g_ffn_pre, w_up, conv_w, conv_b, w_down, g_ffn_post):
    w = _prepare_weights(g_mix_pre, w_in, g_qa, g_ka, g_cq, w_uq, g_ckv, w_ukv, w_oa, w_ob,
                         b_gates, w_out, g_mix_post, g_ffn_pre, w_up, conv_w, conv_b, w_down,
                         g_ffn_post)
    return (_trunk(x_prompt, w), _trunk(x_sample, w))
```

```python
import functools

import jax
import jax.numpy as jnp
import numpy as np
from jax import lax
from jax.experimental import pallas as pl
from jax.experimental.pallas import tpu as pltpu

D_MODEL = 1024
GRID_W = 64
ROPE_THETA = 10000.0
EPS = 1e-6

H_A = 8
KV_A = 2
HD_A = 64
H_B = 8
Q_LORA = 384
KV_LORA = 256
NOPE_B = 64
ROPE_B = 32
V_B = 64
QK_B = NOPE_B + ROPE_B
D_FF = 2816
CONV_W = 3

W_QA = H_A * HD_A
W_KA = KV_A * HD_A
W_OB = H_B * V_B

LANE = 128
HALF = LANE // 2
HALO = 8
VMEM_LIMIT = 56 * 1024 * 1024
LOG2E = 1.4426950408889634

BF16 = jnp.bfloat16
F32 = jnp.float32


def _head_cols_rot(base, dim):
    half = dim // 2
    cols = np.full((LANE,), -1, np.int64)
    cols[:half] = base + np.arange(0, dim, 2)
    cols[HALF:HALF + half] = base + np.arange(1, dim, 2)
    return cols


def _head_cols_plain(base, dim):
    cols = np.full((LANE,), -1, np.int64)
    cols[:dim] = base + np.arange(dim)
    return cols


def _mla_cols(nope_base, rope_base):
    cols = np.full((LANE,), -1, np.int64)
    hn = NOPE_B // 2
    hr = ROPE_B // 2
    if nope_base is not None:
        cols[:hn] = nope_base + np.arange(hn)
        cols[HALF:HALF + hn] = nope_base + hn + np.arange(hn)
    if rope_base is not None:
        cols[hn:hn + hr] = rope_base + np.arange(0, ROPE_B, 2)
        cols[HALF + hn:HALF + hn + hr] = rope_base + np.arange(1, ROPE_B, 2)
    return cols


def _in_proj_cols():
    o_ka = W_QA
    o_va = o_ka + W_KA
    o_cq = o_va + W_KA
    o_ckv = o_cq + Q_LORA
    o_kr = o_ckv + KV_LORA
    blocks = [_head_cols_rot(h * HD_A, HD_A) for h in range(H_A)]
    blocks += [_head_cols_rot(o_ka + j * HD_A, HD_A) for j in range(KV_A)]
    blocks += [_head_cols_plain(o_va + j * HD_A, HD_A) for j in range(KV_A)]
    blocks += [o_cq + np.arange(Q_LORA), o_ckv + np.arange(KV_LORA)]
    blocks += [_mla_cols(None, o_kr)]
    return np.concatenate(blocks), o_kr + ROPE_B


def _uq_cols():
    return np.concatenate([_mla_cols(h * QK_B, h * QK_B + NOPE_B) for h in range(H_B)])


def _ukv_cols():
    stride = NOPE_B + V_B
    k = [_mla_cols(h * stride, None) for h in range(H_B)]
    v = [_head_cols_plain(h * stride + NOPE_B, V_B) for h in range(H_B)]
    return np.concatenate(k + v)


def _gather_cols(w, cols):
    taken = jnp.take(w, jnp.asarray(np.maximum(cols, 0), jnp.int32), axis=-1)
    return jnp.where(jnp.asarray(cols >= 0), taken, 0.0)


def _rot_tables(seq_len):
    rows = seq_len // GRID_W
    row_ids = jnp.repeat(jnp.arange(rows, dtype=F32), GRID_W)
    col_ids = jnp.tile(jnp.arange(GRID_W, dtype=F32), rows)

    def angles(rot_dim):
        half = rot_dim // 2
        inv = ROPE_THETA ** (-jnp.arange(0, half, 2, dtype=F32) / half)
        ang = jnp.concatenate([row_ids[:, None] * inv, col_ids[:, None] * inv], axis=-1)
        return jnp.cos(ang), jnp.sin(ang)

    def place(cos, sin, lo):
        n = cos.shape[1]
        c = jnp.ones((seq_len, LANE), F32)
        s = jnp.zeros((seq_len, LANE), F32)
        c = c.at[:, lo:lo + n].set(cos).at[:, HALF + lo:HALF + lo + n].set(cos)
        s = s.at[:, lo:lo + n].set(-sin).at[:, HALF + lo:HALF + lo + n].set(sin)
        return c, s

    ca, sa = place(*angles(HD_A), 0)
    cb, sb = place(*angles(ROPE_B), NOPE_B // 2)
    return jnp.stack([ca, sa, cb, sb])


def _rms(x, g):
    return x * lax.rsqrt(jnp.mean(x * x, axis=-1, keepdims=True) + EPS) * g


def _rope(y, cos, sin):
    return y * cos + pltpu.roll(y, HALF, axis=1) * sin


def _dot(a, b):
    return jnp.dot(a, b, preferred_element_type=F32)


def _const_spec(shape):
    zeros = (0,) * len(shape)
    return pl.BlockSpec(shape, lambda *_: zeros)


def _params(n_axes, last_arbitrary=False):
    sem = ["parallel"] * n_axes
    if last_arbitrary:
        sem[-1] = "arbitrary"
    return pltpu.CompilerParams(dimension_semantics=tuple(sem), vmem_limit_bytes=VMEM_LIMIT)


def _in_proj_kernel(x_ref, tab_ref, g_pre_ref, w1_ref, gqa_ref, gka_ref, gcq_ref, gckv_ref,
                    wuq_ref, wukv_ref,
                    qa_ref, ka_ref, va_ref, qb_ref, kb_ref, vb_ref):
    ca, sa, cb, sb = tab_ref[0], tab_ref[1], tab_ref[2], tab_ref[3]
    low = lax.broadcasted_iota(jnp.int32, (1, LANE), 1) < HALF

    h = _rms(x_ref[0], g_pre_ref[...]).astype(BF16)
    proj = _dot(h, w1_ref[...])

    def head_norm_rope(blk, g):
        ms = jnp.sum(blk * blk, axis=-1, keepdims=True) * (1.0 / HD_A)
        return _rope(blk * lax.rsqrt(ms + EPS) * g, ca, sa)

    def value_t(blk):
        return jnp.where(low, blk, 1.0).T.astype(BF16)

    off = 0
    for hh in range(H_A):
        qa_ref[0, hh] = head_norm_rope(proj[:, off:off + LANE], gqa_ref[...]).astype(BF16)
        off += LANE
    for j in range(KV_A):
        ka_ref[0, j] = head_norm_rope(proj[:, off:off + LANE], gka_ref[...]).astype(BF16)
        off += LANE
    for j in range(KV_A):
        va_ref[0, j] = value_t(proj[:, off:off + LANE])
        off += LANE

    cq = _rms(proj[:, off:off + Q_LORA], gcq_ref[...]).astype(BF16)
    off += Q_LORA
    ckv = _rms(proj[:, off:off + KV_LORA], gckv_ref[...]).astype(BF16)
    off += KV_LORA
    kr = _rope(proj[:, off:off + LANE], cb, sb)

    qb = _dot(cq, wuq_ref[...])
    kvb = _dot(ckv, wukv_ref[...])
    scale_b = QK_B ** -0.5 * LOG2E
    for hh in range(H_B):
        sl = slice(hh * LANE, (hh + 1) * LANE)
        qb_ref[0, hh] = (_rope(qb[:, sl], cb, sb) * scale_b).astype(BF16)
        kb_ref[0, hh] = (kvb[:, sl] + kr).astype(BF16)
        vb_ref[0, hh] = value_t(kvb[:, H_B * LANE + hh * LANE:H_B * LANE + (hh + 1) * LANE])


def _in_proj(x, tabs, g_pre, w1, gqa, gka, gcq, gckv, wuq, wukv, tm):
    B, S, _ = x.shape
    n1 = w1.shape[1]
    rows = lambda n: pl.BlockSpec((1, n, tm, LANE), lambda b, i: (b, 0, i, 0))
    cols = lambda n: pl.BlockSpec((1, n, LANE, tm), lambda b, i: (b, 0, 0, i))
    rows_shape = lambda n: jax.ShapeDtypeStruct((B, n, S, LANE), BF16)
    cols_shape = lambda n: jax.ShapeDtypeStruct((B, n, LANE, S), BF16)
    return pl.pallas_call(
        _in_proj_kernel,
        grid=(B, S // tm),
        in_specs=[
            pl.BlockSpec((1, tm, D_MODEL), lambda b, i: (b, i, 0)),
            pl.BlockSpec((4, tm, LANE), lambda b, i: (0, i, 0)),
            _const_spec((1, D_MODEL)),
            _const_spec((D_MODEL, n1)),
            _const_spec((1, LANE)), _const_spec((1, LANE)),
            _const_spec((1, Q_LORA)), _const_spec((1, KV_LORA)),
            _const_spec((Q_LORA, H_B * LANE)),
            _const_spec((KV_LORA, 2 * H_B * LANE)),
        ],
        out_specs=[rows(H_A), rows(KV_A), cols(KV_A), rows(H_B), rows(H_B), cols(H_B)],
        out_shape=[rows_shape(H_A), rows_shape(KV_A), cols_shape(KV_A),
                   rows_shape(H_B), rows_shape(H_B), cols_shape(H_B)],
        compiler_params=_params(2),
        name="in_proj",
    )(x, tabs, g_pre, w1, gqa, gka, gcq, gckv, wuq, wukv)


def _attn_kernel(q_ref, k_ref, vt_ref, o_ref, *scratch, n_heads, group, n_chunks, lead):
    kc = pl.program_id(2)

    if n_chunks > 1:
        m_ref, acc_ref = scratch

        @pl.when(kc == 0)
        def _():
            m_ref[...] = jnp.full(m_ref.shape, -jnp.inf, F32)
            acc_ref[...] = jnp.zeros(acc_ref.shape, F32)

    def scores_t(h):
        return lax.dot_general(k_ref[0, h // group], q_ref[0, h], (((1,), (1,)), ((), ())),
                               preferred_element_type=F32)

    def weighted_values(h, s):
        m = jnp.max(s, axis=0, keepdims=True)
        if n_chunks > 1:
            m_prev = m_ref[h]
            m_new = jnp.maximum(m_prev, m)
            alpha = jnp.exp2(m_prev - m_new)[:1]
            m_ref[h] = m_new
            m = m_new[:1]
        p = jnp.exp2(s - m).astype(BF16)
        acc = _dot(vt_ref[0, h // group], p)
        if n_chunks > 1:
            acc = alpha * acc_ref[h] + acc
            acc_ref[h] = acc
        return acc

    accs = [None] * n_heads
    scores = {h: scores_t(h) for h in range(min(lead, n_heads))}
    for h0 in range(0, n_heads, 2):
        for h in (h0, h0 + 1):
            accs[h] = weighted_values(h, scores.pop(h))
        for h in (h0, h0 + 1):
            if h + lead < n_heads:
                scores[h + lead] = scores_t(h + lead)

    def emit():
        for c in range(n_heads // 2):
            pair = []
            for h in (2 * c, 2 * c + 1):
                a = accs[h] if n_chunks == 1 else acc_ref[h]
                pair.append(a[:HALF] * (1.0 / a[HALF:HALF + 1]))
            o_ref[0, :, c * LANE:(c + 1) * LANE] = jnp.concatenate(pair, axis=0).T.astype(o_ref.dtype)

    if n_chunks == 1:
        emit()
    else:
        pl.when(kc == n_chunks - 1)(emit)


def _attention(q, k, vt, tq, tk):
    B, n_heads, S, _ = q.shape
    n_kv = k.shape[1]
    n_chunks = S // tk
    scratch = []
    if n_chunks > 1:
        scratch = [pltpu.VMEM((n_heads, 8, tq), F32), pltpu.VMEM((n_heads, LANE, tq), F32)]
    kernel = functools.partial(_attn_kernel, n_heads=n_heads, group=n_heads // n_kv,
                               n_chunks=n_chunks, lead=4)
    return pl.pallas_call(
        kernel,
        grid=(B, S // tq, n_chunks),
        in_specs=[
            pl.BlockSpec((1, n_heads, tq, LANE), lambda b, i, c: (b, 0, i, 0)),
            pl.BlockSpec((1, n_kv, tk, LANE), lambda b, i, c: (b, 0, c, 0)),
            pl.BlockSpec((1, n_kv, LANE, tk), lambda b, i, c: (b, 0, 0, c)),
        ],
        out_specs=pl.BlockSpec((1, tq, n_heads * HALF), lambda b, i, c: (b, i, 0)),
        out_shape=jax.ShapeDtypeStruct((B, S, n_heads * HALF), BF16),
        scratch_shapes=scratch,
        compiler_params=_params(3, last_arbitrary=True),
        name=f"attention_kv{n_kv}",
    )(q, k, vt)


def _merge_kernel(x_ref, ya_ref, yb_ref, g_pre_ref, wg_ref, bg_ref, woa_ref, wob_ref, wout_ref,
                  g_post_ref, o_ref):
    x = x_ref[0]
    h = _rms(x, g_pre_ref[...]).astype(BF16)
    gates = jax.nn.sigmoid(_dot(h, wg_ref[...]) + bg_ref[...])
    pa = _dot(ya_ref[0], woa_ref[...])
    pb = _dot(yb_ref[0], wob_ref[...])
    merged = (gates[:, :D_MODEL] * pa + gates[:, D_MODEL:] * pb).astype(BF16)
    mixed = _dot(merged, wout_ref[...])
    o_ref[0] = x + _rms(mixed, g_post_ref[...])


def _merge(x, ya, yb, g_pre, wg, bg, woa, wob, wout, g_post, tm):
    B, S, _ = x.shape
    tok = lambda w: pl.BlockSpec((1, tm, w), lambda b, i: (b, i, 0))
    return pl.pallas_call(
        _merge_kernel,
        grid=(B, S // tm),
        in_specs=[
            tok(D_MODEL), tok(W_QA), tok(W_OB),
            _const_spec((1, D_MODEL)),
            _const_spec((D_MODEL, 2 * D_MODEL)), _const_spec((1, 2 * D_MODEL)),
            _const_spec((W_QA, D_MODEL)), _const_spec((W_OB, D_MODEL)),
            _const_spec((D_MODEL, D_MODEL)), _const_spec((1, D_MODEL)),
        ],
        out_specs=tok(D_MODEL),
        out_shape=jax.ShapeDtypeStruct((B, S, D_MODEL), F32),
        compiler_params=_params(2),
        name="gated_merge",
    )(x, ya, yb, g_pre, wg, bg, woa, wob, wout, g_post)


def _ffn_kernel(x_ref, xp_ref, xn_ref, g_pre_ref, wup_ref, cw_ref, cb_ref, wdown_ref, g_post_ref,
                o_ref, f_ref, *, n_chunks):
    tm = x_ref.shape[1]
    i = pl.program_id(1)
    x = x_ref[0]
    xp = jnp.where(i > 0, xp_ref[0, 0], 0.0)
    xn = jnp.where(i < pl.num_programs(1) - 1, xn_ref[0, 0], 0.0)
    xa = jnp.concatenate([xp, x, xn], axis=0)
    h = _rms(xa, g_pre_ref[...]).astype(BF16)
    rows = tm + 2 * HALO
    width = D_FF // n_chunks

    def conv(col0):
        u = _dot(h, wup_ref[:, col0:col0 + width])
        um = pltpu.roll(u, 1, axis=0)[HALO:HALO + tm]
        un = pltpu.roll(u, rows - 1, axis=0)[HALO:HALO + tm]
        cw = cw_ref[:, col0:col0 + width]
        return (um * cw[0:1] + u[HALO:HALO + tm] * cw[1:2] + un * cw[2:3]
                + cb_ref[:, col0:col0 + width])

    for c in range(n_chunks):
        gate = conv(c * width)
        val = conv(D_FF + c * width)
        f_ref[:, c * width:(c + 1) * width] = (jax.nn.gelu(gate, approximate=True) * val).astype(BF16)

    out = _dot(f_ref[...], wdown_ref[...])
    o_ref[0] = x + _rms(out, g_post_ref[...])


def _ffn(x, g_pre, wup, cw, cb, wdown, g_post, tm, n_chunks):
    B, S, _ = x.shape
    x_rows = x.reshape(B, S // HALO, HALO, D_MODEL)
    per = tm // HALO
    last = S // HALO - 1
    tok = pl.BlockSpec((1, tm, D_MODEL), lambda b, i: (b, i, 0))
    kernel = functools.partial(_ffn_kernel, n_chunks=n_chunks)
    return pl.pallas_call(
        kernel,
        grid=(B, S // tm),
        in_specs=[
            tok,
            pl.BlockSpec((1, 1, HALO, D_MODEL), lambda b, i: (b, jnp.maximum(i * per - 1, 0), 0, 0)),
            pl.BlockSpec((1, 1, HALO, D_MODEL), lambda b, i: (b, jnp.minimum((i + 1) * per, last), 0, 0)),
            _const_spec((1, D_MODEL)),
            _const_spec((D_MODEL, 2 * D_FF)),
            _const_spec((CONV_W, 2 * D_FF)), _const_spec((1, 2 * D_FF)),
            _const_spec((D_FF, D_MODEL)), _const_spec((1, D_MODEL)),
        ],
        out_specs=tok,
        out_shape=jax.ShapeDtypeStruct((B, S, D_MODEL), F32),
        scratch_shapes=[pltpu.VMEM((tm, D_FF), BF16)],
        compiler_params=_params(2),
        name="channel_mixer",
    )(x, x_rows, x_rows, g_pre, wup, cw, cb, wdown, g_post)


def _prepare_weights(g_mix_pre, w_in, g_qa, g_ka, g_cq, w_uq, g_ckv, w_ukv, w_oa, w_ob, b_gates,
                     w_out, g_mix_post, g_ffn_pre, w_up, conv_w, conv_b, w_down, g_ffn_post):
    cols1, gate0 = _in_proj_cols()
    rot = _head_cols_rot(0, HD_A)
    row = lambda a: a[:, None, :]
    return dict(
        g_pre=row(g_mix_pre),
        w1=_gather_cols(w_in, cols1).astype(BF16),
        wg=w_in[:, :, gate0:].astype(BF16),
        gqa=row(_gather_cols(g_qa, rot)) * (HD_A ** -0.5 * LOG2E),
        gka=row(_gather_cols(g_ka, rot)),
        gcq=row(g_cq), gckv=row(g_ckv),
        wuq=_gather_cols(w_uq, _uq_cols()).astype(BF16),
        wukv=_gather_cols(w_ukv, _ukv_cols()).astype(BF16),
        woa=w_oa.astype(BF16), wob=w_ob.astype(BF16), bg=row(b_gates),
        wout=w_out.astype(BF16), g_post=row(g_mix_post),
        g_ffn_pre=row(g_ffn_pre), wup=w_up.astype(BF16), cw=conv_w, cb=row(conv_b),
        wdown=w_down.astype(BF16), g_ffn_post=row(g_ffn_post),
    )


def _tiles(seq_len):
    return dict(tm=512, tq_a=256, tq_b=256, tk=min(seq_len, 2048))


def _trunk(x, w):
    S = x.shape[1]
    t = _tiles(S)
    tabs = _rot_tables(S)
    depth = w["w1"].shape[0]
    for l in range(depth):
        p = {k: v[l] for k, v in w.items()}
        qa, ka, va, qb, kb, vb = _in_proj(x, tabs, p["g_pre"], p["w1"], p["gqa"], p["gka"],
                                          p["gcq"], p["gckv"], p["wuq"], p["wukv"], t["tm"])
        ya = _attention(qa, ka, va, t["tq_a"], t["tk"])
        yb = _attention(qb, kb, vb, t["tq_b"], t["tk"])
        x = _merge(x, ya, yb, p["g_pre"], p["wg"], p["bg"], p["woa"], p["wob"], p["wout"],
                   p["g_post"], t["tm"])
        x = _ffn(x, p["g_ffn_pre"], p["wup"], p["cw"], p["cb"], p["wdown"], p["g_ffn_post"],
                 t["tm"], n_chunks=2)
    return x


def kernel(x_prompt, x_sample, g_mix_pre, w_in, g_qa, g_ka, g_cq, w_uq, g_ckv, w_ukv, w_oa, w_ob,
           b_gates, w_out, g_mix_post, g_ffn_pre, w_up, conv_w, conv_b, w_down, g_ffn_post):
    w = _prepare_weights(g_mix_pre, w_in, g_qa, g_ka, g_cq, w_uq, g_ckv, w_ukv, w_oa, w_ob,
                         b_gates, w_out, g_mix_post, g_ffn_pre, w_up, conv_w, conv_b, w_down,
                         g_ffn_post)
    return (_trunk(x_prompt, w), _trunk(x_sample, w))
```

```python
import functools

import jax
import jax.numpy as jnp
import numpy as np
from jax import lax
from jax.experimental import pallas as pl
from jax.experimental.pallas import tpu as pltpu

D_MODEL = 1024
GRID_W = 64
ROPE_THETA = 10000.0
EPS = 1e-6

H_A = 8
KV_A = 2
HD_A = 64
H_B = 8
Q_LORA = 384
KV_LORA = 256
NOPE_B = 64
ROPE_B = 32
V_B = 64
QK_B = NOPE_B + ROPE_B
D_FF = 2816
CONV_W = 3

W_QA = H_A * HD_A
W_KA = KV_A * HD_A
W_OB = H_B * V_B

LANE = 128
HALF = LANE // 2
HALO = 8
VMEM_LIMIT = 56 * 1024 * 1024
LOG2E = 1.4426950408889634

BF16 = jnp.bfloat16
F32 = jnp.float32


def _head_cols_rot(base, dim):
    half = dim // 2
    cols = np.full((LANE,), -1, np.int64)
    cols[:half] = base + np.arange(0, dim, 2)
    cols[HALF:HALF + half] = base + np.arange(1, dim, 2)
    return cols


def _head_cols_plain(base, dim):
    cols = np.full((LANE,), -1, np.int64)
    cols[:dim] = base + np.arange(dim)
    return cols


def _mla_cols(nope_base, rope_base):
    cols = np.full((LANE,), -1, np.int64)
    hn = NOPE_B // 2
    hr = ROPE_B // 2
    if nope_base is not None:
        cols[:hn] = nope_base + np.arange(hn)
        cols[HALF:HALF + hn] = nope_base + hn + np.arange(hn)
    if rope_base is not None:
        cols[hn:hn + hr] = rope_base + np.arange(0, ROPE_B, 2)
        cols[HALF + hn:HALF + hn + hr] = rope_base + np.arange(1, ROPE_B, 2)
    return cols


def _in_proj_cols():
    o_ka = W_QA
    o_va = o_ka + W_KA
    o_cq = o_va + W_KA
    o_ckv = o_cq + Q_LORA
    o_kr = o_ckv + KV_LORA
    blocks = [_head_cols_rot(h * HD_A, HD_A) for h in range(H_A)]
    blocks += [_head_cols_rot(o_ka + j * HD_A, HD_A) for j in range(KV_A)]
    blocks += [_head_cols_plain(o_va + j * HD_A, HD_A) for j in range(KV_A)]
    blocks += [o_cq + np.arange(Q_LORA), o_ckv + np.arange(KV_LORA)]
    blocks += [_mla_cols(None, o_kr)]
    return np.concatenate(blocks), o_kr + ROPE_B


def _uq_cols():
    return np.concatenate([_mla_cols(h * QK_B, h * QK_B + NOPE_B) for h in range(H_B)])


def _ukv_cols():
    stride = NOPE_B + V_B
    k = [_mla_cols(h * stride, None) for h in range(H_B)]
    v = [_head_cols_plain(h * stride + NOPE_B, V_B) for h in range(H_B)]
    return np.concatenate(k + v)


def _gather_cols(w, cols):
    taken = jnp.take(w, jnp.asarray(np.maximum(cols, 0), jnp.int32), axis=-1)
    return jnp.where(jnp.asarray(cols >= 0), taken, 0.0)


def _rot_tables(seq_len):
    rows = seq_len // GRID_W
    row_ids = jnp.repeat(jnp.arange(rows, dtype=F32), GRID_W)
    col_ids = jnp.tile(jnp.arange(GRID_W, dtype=F32), rows)

    def angles(rot_dim):
        half = rot_dim // 2
        inv = ROPE_THETA ** (-jnp.arange(0, half, 2, dtype=F32) / half)
        ang = jnp.concatenate([row_ids[:, None] * inv, col_ids[:, None] * inv], axis=-1)
        return jnp.cos(ang), jnp.sin(ang)

    def place(cos, sin, lo):
        n = cos.shape[1]
        c = jnp.ones((seq_len, LANE), F32)
        s = jnp.zeros((seq_len, LANE), F32)
        c = c.at[:, lo:lo + n].set(cos).at[:, HALF + lo:HALF + lo + n].set(cos)
        s = s.at[:, lo:lo + n].set(-sin).at[:, HALF + lo:HALF + lo + n].set(sin)
        return c, s

    ca, sa = place(*angles(HD_A), 0)
    cb, sb = place(*angles(ROPE_B), NOPE_B // 2)
    return jnp.stack([ca, sa, cb, sb])


def _rms(x, g):
    return x * lax.rsqrt(jnp.mean(x * x, axis=-1, keepdims=True) + EPS) * g


def _rope(y, cos, sin):
    return y * cos + pltpu.roll(y, HALF, axis=1) * sin


def _dot(a, b):
    return jnp.dot(a, b, preferred_element_type=F32)


def _const_spec(shape):
    zeros = (0,) * len(shape)
    return pl.BlockSpec(shape, lambda *_: zeros)


def _params(n_axes, last_arbitrary=False):
    sem = ["parallel"] * n_axes
    if last_arbitrary:
        sem[-1] = "arbitrary"
    return pltpu.CompilerParams(dimension_semantics=tuple(sem), vmem_limit_bytes=VMEM_LIMIT)


def _in_proj_kernel(x_ref, tab_ref, g_pre_ref, w1_ref, gqa_ref, gka_ref, gcq_ref, gckv_ref,
                    wuq_ref, wukv_ref,
                    qa_ref, ka_ref, va_ref, qb_ref, kb_ref, vb_ref, *, sub):
    low = lax.broadcasted_iota(jnp.int32, (1, LANE), 1) < HALF
    scale_b = QK_B ** -0.5 * LOG2E

    def value_t(blk):
        return jnp.where(low, blk, 1.0).T.astype(BF16)

    for r in range(x_ref.shape[1] // sub):
        rs = slice(r * sub, (r + 1) * sub)
        ca, sa, cb, sb = tab_ref[0, rs], tab_ref[1, rs], tab_ref[2, rs], tab_ref[3, rs]

        def head_norm_rope(blk, g):
            ms = jnp.sum(blk * blk, axis=-1, keepdims=True) * (1.0 / HD_A)
            return _rope(blk * lax.rsqrt(ms + EPS) * g, ca, sa)

        h = _rms(x_ref[0, rs], g_pre_ref[...]).astype(BF16)
        proj = _dot(h, w1_ref[...])

        off = 0
        for hh in range(H_A):
            qa_ref[0, hh, rs] = head_norm_rope(proj[:, off:off + LANE], gqa_ref[...]).astype(BF16)
            off += LANE
        for j in range(KV_A):
            ka_ref[0, j, rs] = head_norm_rope(proj[:, off:off + LANE], gka_ref[...]).astype(BF16)
            off += LANE
        for j in range(KV_A):
            va_ref[0, j, :, rs] = value_t(proj[:, off:off + LANE])
            off += LANE

        cq = _rms(proj[:, off:off + Q_LORA], gcq_ref[...]).astype(BF16)
        off += Q_LORA
        ckv = _rms(proj[:, off:off + KV_LORA], gckv_ref[...]).astype(BF16)
        off += KV_LORA
        kr = _rope(proj[:, off:off + LANE], cb, sb)

        qb = _dot(cq, wuq_ref[...])
        kvb = _dot(ckv, wukv_ref[...])
        for hh in range(H_B):
            sl = slice(hh * LANE, (hh + 1) * LANE)
            qb_ref[0, hh, rs] = (_rope(qb[:, sl], cb, sb) * scale_b).astype(BF16)
            kb_ref[0, hh, rs] = (kvb[:, sl] + kr).astype(BF16)
            vb_ref[0, hh, :, rs] = value_t(kvb[:, H_B * LANE + hh * LANE:H_B * LANE + (hh + 1) * LANE])


def _in_proj(x, tabs, g_pre, w1, gqa, gka, gcq, gckv, wuq, wukv, tm):
    B, S, _ = x.shape
    n1 = w1.shape[1]
    rows = lambda n: pl.BlockSpec((1, n, tm, LANE), lambda b, i: (b, 0, i, 0))
    cols = lambda n: pl.BlockSpec((1, n, LANE, tm), lambda b, i: (b, 0, 0, i))
    rows_shape = lambda n: jax.ShapeDtypeStruct((B, n, S, LANE), BF16)
    cols_shape = lambda n: jax.ShapeDtypeStruct((B, n, LANE, S), BF16)
    return pl.pallas_call(
        functools.partial(_in_proj_kernel, sub=min(tm, 256)),
        grid=(B, S // tm),
        in_specs=[
            pl.BlockSpec((1, tm, D_MODEL), lambda b, i: (b, i, 0)),
            pl.BlockSpec((4, tm, LANE), lambda b, i: (0, i, 0)),
            _const_spec((1, D_MODEL)),
            _const_spec((D_MODEL, n1)),
            _const_spec((1, LANE)), _const_spec((1, LANE)),
            _const_spec((1, Q_LORA)), _const_spec((1, KV_LORA)),
            _const_spec((Q_LORA, H_B * LANE)),
            _const_spec((KV_LORA, 2 * H_B * LANE)),
        ],
        out_specs=[rows(H_A), rows(KV_A), cols(KV_A), rows(H_B), rows(H_B), cols(H_B)],
        out_shape=[rows_shape(H_A), rows_shape(KV_A), cols_shape(KV_A),
                   rows_shape(H_B), rows_shape(H_B), cols_shape(H_B)],
        compiler_params=_params(2),
        name="in_proj",
    )(x, tabs, g_pre, w1, gqa, gka, gcq, gckv, wuq, wukv)


def _attn_kernel(q_ref, k_ref, vt_ref, o_ref, *scratch, n_heads, group, n_chunks, lead):
    kc = pl.program_id(2)

    if n_chunks > 1:
        m_ref, acc_ref = scratch

        @pl.when(kc == 0)
        def _():
            m_ref[...] = jnp.full(m_ref.shape, -jnp.inf, F32)
            acc_ref[...] = jnp.zeros(acc_ref.shape, F32)

    def scores_t(h):
        return lax.dot_general(k_ref[0, h // group], q_ref[0, h], (((1,), (1,)), ((), ())),
                               preferred_element_type=F32)

    def weighted_values(h, s):
        m = jnp.max(s, axis=0, keepdims=True)
        if n_chunks > 1:
            m_prev = m_ref[h]
            m_new = jnp.maximum(m_prev, m)
            alpha = jnp.exp2(m_prev - m_new)[:1]
            m_ref[h] = m_new
            m = m_new[:1]
        p = jnp.exp2(s - m).astype(BF16)
        acc = _dot(vt_ref[0, h // group], p)
        if n_chunks > 1:
            acc = alpha * acc_ref[h] + acc
            acc_ref[h] = acc
        return acc

    accs = [None] * n_heads
    scores = {h: scores_t(h) for h in range(min(lead, n_heads))}
    for h0 in range(0, n_heads, 2):
        for h in (h0, h0 + 1):
            accs[h] = weighted_values(h, scores.pop(h))
        for h in (h0, h0 + 1):
            if h + lead < n_heads:
                scores[h + lead] = scores_t(h + lead)

    def emit():
        for c in range(n_heads // 2):
            pair = []
            for h in (2 * c, 2 * c + 1):
                a = accs[h] if n_chunks == 1 else acc_ref[h]
                pair.append(a[:HALF] * (1.0 / a[HALF:HALF + 1]))
            o_ref[0, :, c * LANE:(c + 1) * LANE] = jnp.concatenate(pair, axis=0).T.astype(o_ref.dtype)

    if n_chunks == 1:
        emit()
    else:
        pl.when(kc == n_chunks - 1)(emit)


def _attention(q, k, vt, tq, tk):
    B, n_heads, S, _ = q.shape
    n_kv = k.shape[1]
    n_chunks = S // tk
    scratch = []
    if n_chunks > 1:
        scratch = [pltpu.VMEM((n_heads, 8, tq), F32), pltpu.VMEM((n_heads, LANE, tq), F32)]
    kernel = functools.partial(_attn_kernel, n_heads=n_heads, group=n_heads // n_kv,
                               n_chunks=n_chunks, lead=4)
    return pl.pallas_call(
        kernel,
        grid=(B, S // tq, n_chunks),
        in_specs=[
            pl.BlockSpec((1, n_heads, tq, LANE), lambda b, i, c: (b, 0, i, 0)),
            pl.BlockSpec((1, n_kv, tk, LANE), lambda b, i, c: (b, 0, c, 0)),
            pl.BlockSpec((1, n_kv, LANE, tk), lambda b, i, c: (b, 0, 0, c)),
        ],
        out_specs=pl.BlockSpec((1, tq, n_heads * HALF), lambda b, i, c: (b, i, 0)),
        out_shape=jax.ShapeDtypeStruct((B, S, n_heads * HALF), BF16),
        scratch_shapes=scratch,
        compiler_params=_params(3, last_arbitrary=True),
        name=f"attention_kv{n_kv}",
    )(q, k, vt)


def _merge_kernel(x_ref, ya_ref, yb_ref, g_pre_ref, wg_ref, bg_ref, woa_ref, wob_ref, wout_ref,
                  g_post_ref, o_ref):
    x = x_ref[0]
    h = _rms(x, g_pre_ref[...]).astype(BF16)
    gates = jax.nn.sigmoid(_dot(h, wg_ref[...]) + bg_ref[...])
    pa = _dot(ya_ref[0], woa_ref[...])
    pb = _dot(yb_ref[0], wob_ref[...])
    merged = (gates[:, :D_MODEL] * pa + gates[:, D_MODEL:] * pb).astype(BF16)
    mixed = _dot(merged, wout_ref[...])
    o_ref[0] = x + _rms(mixed, g_post_ref[...])


def _merge(x, ya, yb, g_pre, wg, bg, woa, wob, wout, g_post, tm):
    B, S, _ = x.shape
    tok = lambda w: pl.BlockSpec((1, tm, w), lambda b, i: (b, i, 0))
    return pl.pallas_call(
        _merge_kernel,
        grid=(B, S // tm),
        in_specs=[
            tok(D_MODEL), tok(W_QA), tok(W_OB),
            _const_spec((1, D_MODEL)),
            _const_spec((D_MODEL, 2 * D_MODEL)), _const_spec((1, 2 * D_MODEL)),
            _const_spec((W_QA, D_MODEL)), _const_spec((W_OB, D_MODEL)),
            _const_spec((D_MODEL, D_MODEL)), _const_spec((1, D_MODEL)),
        ],
        out_specs=tok(D_MODEL),
        out_shape=jax.ShapeDtypeStruct((B, S, D_MODEL), F32),
        compiler_params=_params(2),
        name="gated_merge",
    )(x, ya, yb, g_pre, wg, bg, woa, wob, wout, g_post)


def _ffn_kernel(x_ref, xp_ref, xn_ref, g_pre_ref, wup_ref, cw_ref, cb_ref, wdown_ref, g_post_ref,
                o_ref, f_ref, *, n_chunks):
    tm = x_ref.shape[1]
    i = pl.program_id(1)
    x = x_ref[0]
    xp = jnp.where(i > 0, xp_ref[0, 0], 0.0)
    xn = jnp.where(i < pl.num_programs(1) - 1, xn_ref[0, 0], 0.0)
    xa = jnp.concatenate([xp, x, xn], axis=0)
    h = _rms(xa, g_pre_ref[...]).astype(BF16)
    rows = tm + 2 * HALO
    width = D_FF // n_chunks

    def conv(col0):
        u = _dot(h, wup_ref[:, col0:col0 + width])
        um = pltpu.roll(u, 1, axis=0)[HALO:HALO + tm]
        un = pltpu.roll(u, rows - 1, axis=0)[HALO:HALO + tm]
        cw = cw_ref[:, col0:col0 + width]
        return (um * cw[0:1] + u[HALO:HALO + tm] * cw[1:2] + un * cw[2:3]
                + cb_ref[:, col0:col0 + width])

    for c in range(n_chunks):
        gate = conv(c * width)
        val = conv(D_FF + c * width)
        f_ref[:, c * width:(c + 1) * width] = (jax.nn.gelu(gate, approximate=True) * val).astype(BF16)

    out = _dot(f_ref[...], wdown_ref[...])
    o_ref[0] = x + _rms(out, g_post_ref[...])


def _ffn(x, g_pre, wup, cw, cb, wdown, g_post, tm, n_chunks):
    B, S, _ = x.shape
    x_rows = x.reshape(B, S // HALO, HALO, D_MODEL)
    per = tm // HALO
    last = S // HALO - 1
    tok = pl.BlockSpec((1, tm, D_MODEL), lambda b, i: (b, i, 0))
    kernel = functools.partial(_ffn_kernel, n_chunks=n_chunks)
    return pl.pallas_call(
        kernel,
        grid=(B, S // tm),
        in_specs=[
            tok,
            pl.BlockSpec((1, 1, HALO, D_MODEL), lambda b, i: (b, jnp.maximum(i * per - 1, 0), 0, 0)),
            pl.BlockSpec((1, 1, HALO, D_MODEL), lambda b, i: (b, jnp.minimum((i + 1) * per, last), 0, 0)),
            _const_spec((1, D_MODEL)),
            _const_spec((D_MODEL, 2 * D_FF)),
            _const_spec((CONV_W, 2 * D_FF)), _const_spec((1, 2 * D_FF)),
            _const_spec((D_FF, D_MODEL)), _const_spec((1, D_MODEL)),
        ],
        out_specs=tok,
        out_shape=jax.ShapeDtypeStruct((B, S, D_MODEL), F32),
        scratch_shapes=[pltpu.VMEM((tm, D_FF), BF16)],
        compiler_params=_params(2),
        name="channel_mixer",
    )(x, x_rows, x_rows, g_pre, wup, cw, cb, wdown, g_post)


def _prepare_weights(g_mix_pre, w_in, g_qa, g_ka, g_cq, w_uq, g_ckv, w_ukv, w_oa, w_ob, b_gates,
                     w_out, g_mix_post, g_ffn_pre, w_up, conv_w, conv_b, w_down, g_ffn_post):
    cols1, gate0 = _in_proj_cols()
    rot = _head_cols_rot(0, HD_A)
    row = lambda a: a[:, None, :]
    return dict(
        g_pre=row(g_mix_pre),
        w1=_gather_cols(w_in, cols1).astype(BF16),
        wg=w_in[:, :, gate0:].astype(BF16),
        gqa=row(_gather_cols(g_qa, rot)) * (HD_A ** -0.5 * LOG2E),
        gka=row(_gather_cols(g_ka, rot)),
        gcq=row(g_cq), gckv=row(g_ckv),
        wuq=_gather_cols(w_uq, _uq_cols()).astype(BF16),
        wukv=_gather_cols(w_ukv, _ukv_cols()).astype(BF16),
        woa=w_oa.astype(BF16), wob=w_ob.astype(BF16), bg=row(b_gates),
        wout=w_out.astype(BF16), g_post=row(g_mix_post),
        g_ffn_pre=row(g_ffn_pre), wup=w_up.astype(BF16), cw=conv_w, cb=row(conv_b),
        wdown=w_down.astype(BF16), g_ffn_post=row(g_ffn_post),
    )


def _tiles(seq_len):
    return dict(tm=512, tq_a=512, tq_b=512, tk=min(seq_len, 2048))


def _trunk(x, w):
    S = x.shape[1]
    t = _tiles(S)
    tabs = _rot_tables(S)
    depth = w["w1"].shape[0]
    for l in range(depth):
        p = {k: v[l] for k, v in w.items()}
        qa, ka, va, qb, kb, vb = _in_proj(x, tabs, p["g_pre"], p["w1"], p["gqa"], p["gka"],
                                          p["gcq"], p["gckv"], p["wuq"], p["wukv"], t["tm"])
        ya = _attention(qa, ka, va, t["tq_a"], t["tk"])
        yb = _attention(qb, kb, vb, t["tq_b"], t["tk"])
        x = _merge(x, ya, yb, p["g_pre"], p["wg"], p["bg"], p["woa"], p["wob"], p["wout"],
                   p["g_post"], t["tm"])
        x = _ffn(x, p["g_ffn_pre"], p["wup"], p["cw"], p["cb"], p["wdown"], p["g_ffn_post"],
                 t["tm"], n_chunks=11)
    return x


def kernel(x_prompt, x_sample, g_mix_pre, w_in, g_qa, g_ka, g_cq, w_uq, g_ckv, w_ukv, w_oa, w_ob,
           b_gates, w_out, g_mix_post, g_ffn_pre, w_up, conv_w, conv_b, w_down, g_ffn_post):
    w = _prepare_weights(g_mix_pre, w_in, g_qa, g_ka, g_cq, w_uq, g_ckv, w_ukv, w_oa, w_ob,
                         b_gates, w_out, g_mix_post, g_ffn_pre, w_up, conv_w, conv_b, w_down,
                         g_ffn_post)
    return (_trunk(x_prompt, w), _trunk(x_sample, w))
```

```python
import functools

import jax
import jax.numpy as jnp
import numpy as np
from jax import lax
from jax.experimental import pallas as pl
from jax.experimental.pallas import tpu as pltpu

D_MODEL = 1024
GRID_W = 64
ROPE_THETA = 10000.0
EPS = 1e-6

H_A = 8
KV_A = 2
HD_A = 64
H_B = 8
Q_LORA = 384
KV_LORA = 256
NOPE_B = 64
ROPE_B = 32
V_B = 64
QK_B = NOPE_B + ROPE_B
D_FF = 2816
CONV_W = 3

W_QA = H_A * HD_A
W_KA = KV_A * HD_A
W_OB = H_B * V_B

LANE = 128
HALF = LANE // 2
HALO = 8
VMEM_LIMIT = 56 * 1024 * 1024
LOG2E = 1.4426950408889634

BF16 = jnp.bfloat16
F32 = jnp.float32


def _head_cols_rot(base, dim):
    half = dim // 2
    cols = np.full((LANE,), -1, np.int64)
    cols[:half] = base + np.arange(0, dim, 2)
    cols[HALF:HALF + half] = base + np.arange(1, dim, 2)
    return cols


def _head_cols_plain(base, dim):
    cols = np.full((LANE,), -1, np.int64)
    cols[:dim] = base + np.arange(dim)
    return cols


def _mla_cols(nope_base, rope_base):
    cols = np.full((LANE,), -1, np.int64)
    hn = NOPE_B // 2
    hr = ROPE_B // 2
    if nope_base is not None:
        cols[:hn] = nope_base + np.arange(hn)
        cols[HALF:HALF + hn] = nope_base + hn + np.arange(hn)
    if rope_base is not None:
        cols[hn:hn + hr] = rope_base + np.arange(0, ROPE_B, 2)
        cols[HALF + hn:HALF + hn + hr] = rope_base + np.arange(1, ROPE_B, 2)
    return cols


def _in_proj_cols():
    o_ka = W_QA
    o_va = o_ka + W_KA
    o_cq = o_va + W_KA
    o_ckv = o_cq + Q_LORA
    o_kr = o_ckv + KV_LORA
    blocks = [_head_cols_rot(h * HD_A, HD_A) for h in range(H_A)]
    blocks += [_head_cols_rot(o_ka + j * HD_A, HD_A) for j in range(KV_A)]
    blocks += [_head_cols_plain(o_va + j * HD_A, HD_A) for j in range(KV_A)]
    blocks += [o_cq + np.arange(Q_LORA), o_ckv + np.arange(KV_LORA)]
    blocks += [_mla_cols(None, o_kr)]
    return np.concatenate(blocks), o_kr + ROPE_B


def _uq_cols():
    return np.concatenate([_mla_cols(h * QK_B, h * QK_B + NOPE_B) for h in range(H_B)])


def _ukv_cols():
    stride = NOPE_B + V_B
    k = [_mla_cols(h * stride, None) for h in range(H_B)]
    v = [_head_cols_plain(h * stride + NOPE_B, V_B) for h in range(H_B)]
    return np.concatenate(k + v)


def _gather_cols(w, cols):
    taken = jnp.take(w, jnp.asarray(np.maximum(cols, 0), jnp.int32), axis=-1)
    return jnp.where(jnp.asarray(cols >= 0), taken, 0.0)


def _rot_tables(seq_len):
    pos = jnp.arange(seq_len, dtype=jnp.int32)
    row_id = (pos // GRID_W).astype(F32)[:, None]
    col_id = (pos % GRID_W).astype(F32)[:, None]

    def tables(rot_dim, lo):
        half = rot_dim // 2
        inv = ROPE_THETA ** (-jnp.arange(0, half, 2, dtype=F32) / half)
        freq = np.zeros((LANE,), np.int32)
        use_row = np.zeros((LANE,), bool)
        sign = np.zeros((LANE,), np.float32)
        for base, sg in ((lo, -1.0), (HALF + lo, 1.0)):
            lanes = base + np.arange(half)
            freq[lanes] = np.arange(half) % (half // 2)
            use_row[lanes] = np.arange(half) < half // 2
            sign[lanes] = sg
        ang = jnp.where(use_row, row_id, col_id) * inv[freq]
        rot = sign != 0.0
        return jnp.where(rot, jnp.cos(ang), 1.0), sign * jnp.sin(ang)

    ca, sa = tables(HD_A, 0)
    cb, sb = tables(ROPE_B, NOPE_B // 2)
    return jnp.stack([ca, sa, cb, sb])


def _rms(x, g):
    return x * lax.rsqrt(jnp.mean(x * x, axis=-1, keepdims=True) + EPS) * g


def _rope(y, cos, sin):
    return y * cos + pltpu.roll(y, HALF, axis=1) * sin


def _dot(a, b):
    return jnp.dot(a, b, preferred_element_type=F32)


def _const_spec(shape):
    zeros = (0,) * len(shape)
    return pl.BlockSpec(shape, lambda *_: zeros, pipeline_mode=pl.Buffered(1))


def _params(n_axes, last_arbitrary=False):
    sem = ["parallel"] * n_axes
    if last_arbitrary:
        sem[-1] = "arbitrary"
    return pltpu.CompilerParams(dimension_semantics=tuple(sem), vmem_limit_bytes=VMEM_LIMIT)


def _in_proj_kernel(x_ref, tab_ref, g_pre_ref, w1_ref, gqa_ref, gka_ref, gcq_ref, gckv_ref,
                    wuq_ref, wukv_ref,
                    qa_ref, ka_ref, va_ref, qb_ref, kb_ref, vb_ref, *, sub):
    low = lax.broadcasted_iota(jnp.int32, (1, LANE), 1) < HALF
    scale_b = QK_B ** -0.5 * LOG2E

    def value_t(blk):
        return jnp.where(low, blk, 1.0).T.astype(BF16)

    for r in range(x_ref.shape[1] // sub):
        rs = slice(r * sub, (r + 1) * sub)
        ca, sa, cb, sb = tab_ref[0, rs], tab_ref[1, rs], tab_ref[2, rs], tab_ref[3, rs]

        def head_norm_rope(blk, g):
            ms = jnp.sum(blk * blk, axis=-1, keepdims=True) * (1.0 / HD_A)
            return _rope(blk * lax.rsqrt(ms + EPS) * g, ca, sa)

        h = _rms(x_ref[0, rs], g_pre_ref[...]).astype(BF16)
        proj = _dot(h, w1_ref[...])

        off = 0
        for hh in range(H_A):
            qa_ref[0, hh, rs] = head_norm_rope(proj[:, off:off + LANE], gqa_ref[...]).astype(BF16)
            off += LANE
        for j in range(KV_A):
            ka_ref[0, j, rs] = head_norm_rope(proj[:, off:off + LANE], gka_ref[...]).astype(BF16)
            off += LANE
        for j in range(KV_A):
            va_ref[0, j, :, rs] = value_t(proj[:, off:off + LANE])
            off += LANE

        cq = _rms(proj[:, off:off + Q_LORA], gcq_ref[...]).astype(BF16)
        off += Q_LORA
        ckv = _rms(proj[:, off:off + KV_LORA], gckv_ref[...]).astype(BF16)
        off += KV_LORA
        kr = _rope(proj[:, off:off + LANE], cb, sb)

        qb = _dot(cq, wuq_ref[...])
        kvb = _dot(ckv, wukv_ref[...])
        for hh in range(H_B):
            sl = slice(hh * LANE, (hh + 1) * LANE)
            qb_ref[0, hh, rs] = (_rope(qb[:, sl], cb, sb) * scale_b).astype(BF16)
            kb_ref[0, hh, rs] = (kvb[:, sl] + kr).astype(BF16)
            vb_ref[0, hh, :, rs] = value_t(kvb[:, H_B * LANE + hh * LANE:H_B * LANE + (hh + 1) * LANE])


def _in_proj(x, tabs, g_pre, w1, gqa, gka, gcq, gckv, wuq, wukv, tm):
    B, S, _ = x.shape
    n1 = w1.shape[1]
    rows = lambda n: pl.BlockSpec((1, n, tm, LANE), lambda b, i: (b, 0, i, 0))
    cols = lambda n: pl.BlockSpec((1, n, LANE, tm), lambda b, i: (b, 0, 0, i))
    rows_shape = lambda n: jax.ShapeDtypeStruct((B, n, S, LANE), BF16)
    cols_shape = lambda n: jax.ShapeDtypeStruct((B, n, LANE, S), BF16)
    return pl.pallas_call(
        functools.partial(_in_proj_kernel, sub=min(tm, 256)),
        grid=(B, S // tm),
        in_specs=[
            pl.BlockSpec((1, tm, D_MODEL), lambda b, i: (b, i, 0)),
            pl.BlockSpec((4, tm, LANE), lambda b, i: (0, i, 0)),
            _const_spec((1, D_MODEL)),
            _const_spec((D_MODEL, n1)),
            _const_spec((1, LANE)), _const_spec((1, LANE)),
            _const_spec((1, Q_LORA)), _const_spec((1, KV_LORA)),
            _const_spec((Q_LORA, H_B * LANE)),
            _const_spec((KV_LORA, 2 * H_B * LANE)),
        ],
        out_specs=[rows(H_A), rows(KV_A), cols(KV_A), rows(H_B), rows(H_B), cols(H_B)],
        out_shape=[rows_shape(H_A), rows_shape(KV_A), cols_shape(KV_A),
                   rows_shape(H_B), rows_shape(H_B), cols_shape(H_B)],
        compiler_params=_params(2),
        name="in_proj",
    )(x, tabs, g_pre, w1, gqa, gka, gcq, gckv, wuq, wukv)


def _attn_kernel(q_ref, k_ref, vt_ref, o_ref, *scratch, n_heads, group, n_chunks, lead, col_split,
                 batch):
    kc = pl.program_id(2)

    if n_chunks > 1:
        m_ref, acc_ref = scratch

        @pl.when(kc == 0)
        def _():
            m_ref[...] = jnp.full(m_ref.shape, -jnp.inf, F32)
            acc_ref[...] = jnp.zeros(acc_ref.shape, F32)

    tq = q_ref.shape[2]
    width = tq // col_split
    n_units = n_heads * col_split

    def cols(u):
        return slice((u % col_split) * width, (u % col_split + 1) * width)

    def scores_t(u):
        h = u // col_split
        return lax.dot_general(k_ref[0, h // group], q_ref[0, h, cols(u)], (((1,), (1,)), ((), ())),
                               preferred_element_type=F32)

    def weighted_values(u, s):
        h = u // col_split
        m = jnp.max(s, axis=0, keepdims=True)
        if n_chunks > 1:
            m_prev = m_ref[h, :, cols(u)]
            m_new = jnp.maximum(m_prev, m)
            alpha = jnp.exp2(m_prev - m_new)[:1]
            m_ref[h, :, cols(u)] = m_new
            m = m_new[:1]
        p = jnp.exp2(s - m).astype(BF16)
        acc = _dot(vt_ref[0, h // group], p)
        if n_chunks > 1:
            acc = alpha * acc_ref[h, :, cols(u)] + acc
            acc_ref[h, :, cols(u)] = acc
        return acc

    accs = [None] * n_units
    scores = {u: scores_t(u) for u in range(min(lead, n_units))}
    for u0 in range(0, n_units, batch):
        for u in range(u0, u0 + batch):
            accs[u] = weighted_values(u, scores.pop(u))
        for u in range(u0, u0 + batch):
            if u + lead < n_units:
                scores[u + lead] = scores_t(u + lead)

    def emit():
        for c in range(n_heads // 2):
            pair = []
            for h in (2 * c, 2 * c + 1):
                if n_chunks == 1:
                    a = jnp.concatenate(accs[h * col_split:(h + 1) * col_split], axis=1)
                else:
                    a = acc_ref[h]
                pair.append(a[:HALF] * (1.0 / a[HALF:HALF + 1]))
            o_ref[0, :, c * LANE:(c + 1) * LANE] = jnp.concatenate(pair, axis=0).T.astype(o_ref.dtype)

    if n_chunks == 1:
        emit()
    else:
        pl.when(kc == n_chunks - 1)(emit)


def _attention(q, k, vt, tq, tk):
    B, n_heads, S, _ = q.shape
    n_kv = k.shape[1]
    n_chunks = S // tk
    scratch = []
    if n_chunks > 1:
        scratch = [pltpu.VMEM((n_heads, 8, tq), F32), pltpu.VMEM((n_heads, LANE, tq), F32)]
    kernel = functools.partial(_attn_kernel, n_heads=n_heads, group=n_heads // n_kv,
                               n_chunks=n_chunks, lead=4, col_split=1, batch=2)
    return pl.pallas_call(
        kernel,
        grid=(B, S // tq, n_chunks),
        in_specs=[
            pl.BlockSpec((1, n_heads, tq, LANE), lambda b, i, c: (b, 0, i, 0)),
            pl.BlockSpec((1, n_kv, tk, LANE), lambda b, i, c: (b, 0, c, 0)),
            pl.BlockSpec((1, n_kv, LANE, tk), lambda b, i, c: (b, 0, 0, c)),
        ],
        out_specs=pl.BlockSpec((1, tq, n_heads * HALF), lambda b, i, c: (b, i, 0)),
        out_shape=jax.ShapeDtypeStruct((B, S, n_heads * HALF), BF16),
        scratch_shapes=scratch,
        compiler_params=_params(3, last_arbitrary=True),
        name=f"attention_kv{n_kv}",
    )(q, k, vt)


def _merge_kernel(x_ref, ya_ref, yb_ref, g_pre_ref, wg_ref, bg_ref, woa_ref, wob_ref, wout_ref,
                  g_post_ref, o_ref):
    x = x_ref[0]
    h = _rms(x, g_pre_ref[...]).astype(BF16)
    gates = jax.nn.sigmoid(_dot(h, wg_ref[...]) + bg_ref[...])
    pa = _dot(ya_ref[0], woa_ref[...])
    pb = _dot(yb_ref[0], wob_ref[...])
    merged = (gates[:, :D_MODEL] * pa + gates[:, D_MODEL:] * pb).astype(BF16)
    mixed = _dot(merged, wout_ref[...])
    o_ref[0] = x + _rms(mixed, g_post_ref[...])


def _merge(x, ya, yb, g_pre, wg, bg, woa, wob, wout, g_post, tm):
    B, S, _ = x.shape
    tok = lambda w: pl.BlockSpec((1, tm, w), lambda b, i: (b, i, 0))
    return pl.pallas_call(
        _merge_kernel,
        grid=(B, S // tm),
        in_specs=[
            tok(D_MODEL), tok(W_QA), tok(W_OB),
            _const_spec((1, D_MODEL)),
            _const_spec((D_MODEL, 2 * D_MODEL)), _const_spec((1, 2 * D_MODEL)),
            _const_spec((W_QA, D_MODEL)), _const_spec((W_OB, D_MODEL)),
            _const_spec((D_MODEL, D_MODEL)), _const_spec((1, D_MODEL)),
        ],
        out_specs=tok(D_MODEL),
        out_shape=jax.ShapeDtypeStruct((B, S, D_MODEL), F32),
        compiler_params=_params(2),
        name="gated_merge",
    )(x, ya, yb, g_pre, wg, bg, woa, wob, wout, g_post)


def _ffn_kernel(x_ref, xp_ref, xn_ref, g_pre_ref, wup_ref, cw_ref, cb_ref, wdown_ref, g_post_ref,
                o_ref, f_ref, *, n_chunks):
    tm = x_ref.shape[1]
    i = pl.program_id(1)
    x = x_ref[0]
    xp = jnp.where(i > 0, xp_ref[0], 0.0)
    xn = jnp.where(i < pl.num_programs(1) - 1, xn_ref[0], 0.0)
    xa = jnp.concatenate([xp, x, xn], axis=0)
    h = _rms(xa, g_pre_ref[...]).astype(BF16)
    rows = tm + 2 * HALO
    width = D_FF // n_chunks

    def conv(col0):
        u = _dot(h, wup_ref[:, col0:col0 + width])
        um = pltpu.roll(u, 1, axis=0)[HALO:HALO + tm]
        un = pltpu.roll(u, rows - 1, axis=0)[HALO:HALO + tm]
        cw = cw_ref[:, col0:col0 + width]
        return (um * cw[0:1] + u[HALO:HALO + tm] * cw[1:2] + un * cw[2:3]
                + cb_ref[:, col0:col0 + width])

    for c in range(n_chunks):
        gate = conv(c * width)
        val = conv(D_FF + c * width)
        f_ref[:, c * width:(c + 1) * width] = (jax.nn.gelu(gate, approximate=True) * val).astype(BF16)

    out = _dot(f_ref[...], wdown_ref[...])
    o_ref[0] = x + _rms(out, g_post_ref[...])


def _ffn(x, g_pre, wup, cw, cb, wdown, g_post, tm, n_chunks):
    B, S, _ = x.shape
    per = tm // HALO
    last = S // HALO - 1
    tok = pl.BlockSpec((1, tm, D_MODEL), lambda b, i: (b, i, 0))
    kernel = functools.partial(_ffn_kernel, n_chunks=n_chunks)
    return pl.pallas_call(
        kernel,
        grid=(B, S // tm),
        in_specs=[
            tok,
            pl.BlockSpec((1, HALO, D_MODEL), lambda b, i: (b, jnp.maximum(i * per - 1, 0), 0)),
            pl.BlockSpec((1, HALO, D_MODEL), lambda b, i: (b, jnp.minimum((i + 1) * per, last), 0)),
            _const_spec((1, D_MODEL)),
            _const_spec((D_MODEL, 2 * D_FF)),
            _const_spec((CONV_W, 2 * D_FF)), _const_spec((1, 2 * D_FF)),
            _const_spec((D_FF, D_MODEL)), _const_spec((1, D_MODEL)),
        ],
        out_specs=tok,
        out_shape=jax.ShapeDtypeStruct((B, S, D_MODEL), F32),
        scratch_shapes=[pltpu.VMEM((tm, D_FF), BF16)],
        compiler_params=_params(2),
        name="channel_mixer",
    )(x, x, x, g_pre, wup, cw, cb, wdown, g_post)


def _prepare_weights(g_mix_pre, w_in, g_qa, g_ka, g_cq, w_uq, g_ckv, w_ukv, w_oa, w_ob, b_gates,
                     w_out, g_mix_post, g_ffn_pre, w_up, conv_w, conv_b, w_down, g_ffn_post):
    cols1, gate0 = _in_proj_cols()
    rot = _head_cols_rot(0, HD_A)
    row = lambda a: a[:, None, :]
    return dict(
        g_pre=row(g_mix_pre),
        w1=_gather_cols(w_in, cols1).astype(BF16),
        wg=w_in[:, :, gate0:].astype(BF16),
        gqa=row(_gather_cols(g_qa, rot)) * (HD_A ** -0.5 * LOG2E),
        gka=row(_gather_cols(g_ka, rot)),
        gcq=row(g_cq), gckv=row(g_ckv),
        wuq=_gather_cols(w_uq, _uq_cols()).astype(BF16),
        wukv=_gather_cols(w_ukv, _ukv_cols()).astype(BF16),
        woa=w_oa.astype(BF16), wob=w_ob.astype(BF16), bg=row(b_gates),
        wout=w_out.astype(BF16), g_post=row(g_mix_post),
        g_ffn_pre=row(g_ffn_pre), wup=w_up.astype(BF16), cw=conv_w, cb=row(conv_b),
        wdown=w_down.astype(BF16), g_ffn_post=row(g_ffn_post),
    )


def _tiles(seq_len):
    return dict(tm=1024, tq_a=512, tq_b=512, tk=min(seq_len, 2048))


def _trunk(x, w):
    S = x.shape[1]
    t = _tiles(S)
    tabs = _rot_tables(S)
    depth = w["w1"].shape[0]
    for l in range(depth):
        p = {k: v[l] for k, v in w.items()}
        qa, ka, va, qb, kb, vb = _in_proj(x, tabs, p["g_pre"], p["w1"], p["gqa"], p["gka"],
                                          p["gcq"], p["gckv"], p["wuq"], p["wukv"], t["tm"])
        ya = _attention(qa, ka, va, t["tq_a"], t["tk"])
        yb = _attention(qb, kb, vb, t["tq_b"], t["tk"])
        x = _merge(x, ya, yb, p["g_pre"], p["wg"], p["bg"], p["woa"], p["wob"], p["wout"],
                   p["g_post"], t["tm"])
        x = _ffn(x, p["g_ffn_pre"], p["wup"], p["cw"], p["cb"], p["wdown"], p["g_ffn_post"],
                 t["tm"], n_chunks=11)
    return x


def kernel(x_prompt, x_sample, g_mix_pre, w_in, g_qa, g_ka, g_cq, w_uq, g_ckv, w_ukv, w_oa, w_ob,
           b_gates, w_out, g_mix_post, g_ffn_pre, w_up, conv_w, conv_b, w_down, g_ffn_post):
    w = _prepare_weights(g_mix_pre, w_in, g_qa, g_ka, g_cq, w_uq, g_ckv, w_ukv, w_oa, w_ob,
                         b_gates, w_out, g_mix_post, g_ffn_pre, w_up, conv_w, conv_b, w_down,
                         g_ffn_post)
    return (_trunk(x_prompt, w), _trunk(x_sample, w))
```

```python
import functools

import jax
import jax.numpy as jnp
import numpy as np
from jax import lax
from jax.experimental import pallas as pl
from jax.experimental.pallas import tpu as pltpu

D_MODEL = 1024
GRID_W = 64
ROPE_THETA = 10000.0
EPS = 1e-6

H_A = 8
KV_A = 2
HD_A = 64
H_B = 8
Q_LORA = 384
KV_LORA = 256
NOPE_B = 64
ROPE_B = 32
V_B = 64
QK_B = NOPE_B + ROPE_B
D_FF = 2816
CONV_W = 3

W_QA = H_A * HD_A
W_KA = KV_A * HD_A
W_OB = H_B * V_B

LANE = 128
HALF = LANE // 2
HALO = 8
VMEM_LIMIT = 56 * 1024 * 1024
LOG2E = 1.4426950408889634

BF16 = jnp.bfloat16
F32 = jnp.float32


def _head_cols_rot(base, dim):
    half = dim // 2
    cols = np.full((LANE,), -1, np.int64)
    cols[:half] = base + np.arange(0, dim, 2)
    cols[HALF:HALF + half] = base + np.arange(1, dim, 2)
    return cols


def _head_cols_plain(base, dim):
    cols = np.full((LANE,), -1, np.int64)
    cols[:dim] = base + np.arange(dim)
    return cols


def _mla_cols(nope_base, rope_base):
    cols = np.full((LANE,), -1, np.int64)
    hn = NOPE_B // 2
    hr = ROPE_B // 2
    if nope_base is not None:
        cols[:hn] = nope_base + np.arange(hn)
        cols[HALF:HALF + hn] = nope_base + hn + np.arange(hn)
    if rope_base is not None:
        cols[hn:hn + hr] = rope_base + np.arange(0, ROPE_B, 2)
        cols[HALF + hn:HALF + hn + hr] = rope_base + np.arange(1, ROPE_B, 2)
    return cols


def _in_proj_cols():
    o_ka = W_QA
    o_va = o_ka + W_KA
    o_cq = o_va + W_KA
    o_ckv = o_cq + Q_LORA
    o_kr = o_ckv + KV_LORA
    blocks = [_head_cols_rot(h * HD_A, HD_A) for h in range(H_A)]
    blocks += [_head_cols_rot(o_ka + j * HD_A, HD_A) for j in range(KV_A)]
    blocks += [_head_cols_plain(o_va + j * HD_A, HD_A) for j in range(KV_A)]
    blocks += [o_cq + np.arange(Q_LORA), o_ckv + np.arange(KV_LORA)]
    blocks += [_mla_cols(None, o_kr)]
    return np.concatenate(blocks), o_kr + ROPE_B


def _uq_cols():
    return np.concatenate([_mla_cols(h * QK_B, h * QK_B + NOPE_B) for h in range(H_B)])


def _ukv_cols():
    stride = NOPE_B + V_B
    k = [_mla_cols(h * stride, None) for h in range(H_B)]
    v = [_head_cols_plain(h * stride + NOPE_B, V_B) for h in range(H_B)]
    return np.concatenate(k + v)


def _gather_cols(w, cols):
    taken = jnp.take(w, jnp.asarray(np.maximum(cols, 0), jnp.int32), axis=-1)
    return jnp.where(jnp.asarray(cols >= 0), taken, 0.0)


def _rot_tables(seq_len):
    pos = jnp.arange(seq_len, dtype=jnp.int32)
    row_id = (pos // GRID_W).astype(F32)[:, None]
    col_id = (pos % GRID_W).astype(F32)[:, None]

    def tables(rot_dim, lo):
        half = rot_dim // 2
        inv = ROPE_THETA ** (-jnp.arange(0, half, 2, dtype=F32) / half)
        freq = np.zeros((LANE,), np.int32)
        use_row = np.zeros((LANE,), bool)
        sign = np.zeros((LANE,), np.float32)
        for base, sg in ((lo, -1.0), (HALF + lo, 1.0)):
            lanes = base + np.arange(half)
            freq[lanes] = np.arange(half) % (half // 2)
            use_row[lanes] = np.arange(half) < half // 2
            sign[lanes] = sg
        ang = jnp.where(use_row, row_id, col_id) * inv[freq]
        rot = sign != 0.0
        return jnp.where(rot, jnp.cos(ang), 1.0), sign * jnp.sin(ang)

    ca, sa = tables(HD_A, 0)
    cb, sb = tables(ROPE_B, NOPE_B // 2)
    return jnp.stack([ca, sa, cb, sb])


def _rms(x, g):
    return x * lax.rsqrt(jnp.mean(x * x, axis=-1, keepdims=True) + EPS) * g


def _rope(y, cos, sin):
    return y * cos + pltpu.roll(y, HALF, axis=1) * sin


def _dot(a, b):
    return jnp.dot(a, b, preferred_element_type=F32)


def _const_spec(shape):
    zeros = (0,) * len(shape)
    return pl.BlockSpec(shape, lambda *_: zeros, pipeline_mode=pl.Buffered(1))


def _params(n_axes, last_arbitrary=False):
    sem = ["parallel"] * n_axes
    if last_arbitrary:
        sem[-1] = "arbitrary"
    return pltpu.CompilerParams(dimension_semantics=tuple(sem), vmem_limit_bytes=VMEM_LIMIT)


def _in_proj_kernel(x_ref, tab_ref, g_pre_ref, w1_ref, gqa_ref, gka_ref, gcq_ref, gckv_ref,
                    wuq_ref, wukv_ref,
                    qa_ref, ka_ref, va_ref, qb_ref, kb_ref, vb_ref, *, sub):
    low = lax.broadcasted_iota(jnp.int32, (1, LANE), 1) < HALF
    scale_b = QK_B ** -0.5 * LOG2E

    def value_t(blk):
        return jnp.where(low, blk, 1.0).T.astype(BF16)

    for r in range(x_ref.shape[1] // sub):
        rs = slice(r * sub, (r + 1) * sub)
        ca, sa, cb, sb = tab_ref[0, rs], tab_ref[1, rs], tab_ref[2, rs], tab_ref[3, rs]

        def head_norm_rope(blk, g):
            ms = jnp.sum(blk * blk, axis=-1, keepdims=True) * (1.0 / HD_A)
            return _rope(blk * lax.rsqrt(ms + EPS) * g, ca, sa)

        h = _rms(x_ref[0, rs], g_pre_ref[...]).astype(BF16)
        proj = _dot(h, w1_ref[...])

        off = 0
        for hh in range(H_A):
            qa_ref[0, hh, rs] = head_norm_rope(proj[:, off:off + LANE], gqa_ref[...]).astype(BF16)
            off += LANE
        for j in range(KV_A):
            ka_ref[0, j, rs] = head_norm_rope(proj[:, off:off + LANE], gka_ref[...]).astype(BF16)
            off += LANE
        for j in range(KV_A):
            va_ref[0, j, :, rs] = value_t(proj[:, off:off + LANE])
            off += LANE

        cq = _rms(proj[:, off:off + Q_LORA], gcq_ref[...]).astype(BF16)
        off += Q_LORA
        ckv = _rms(proj[:, off:off + KV_LORA], gckv_ref[...]).astype(BF16)
        off += KV_LORA
        kr = _rope(proj[:, off:off + LANE], cb, sb)

        qb = _dot(cq, wuq_ref[...])
        kvb = _dot(ckv, wukv_ref[...])
        for hh in range(H_B):
            sl = slice(hh * LANE, (hh + 1) * LANE)
            qb_ref[0, hh, rs] = (_rope(qb[:, sl], cb, sb) * scale_b).astype(BF16)
            kb_ref[0, hh, rs] = (kvb[:, sl] + kr).astype(BF16)
            vb_ref[0, hh, :, rs] = value_t(kvb[:, H_B * LANE + hh * LANE:H_B * LANE + (hh + 1) * LANE])


def _in_proj(x, tabs, g_pre, w1, gqa, gka, gcq, gckv, wuq, wukv, tm):
    B, S, _ = x.shape
    n1 = w1.shape[1]
    rows = lambda n: pl.BlockSpec((1, n, tm, LANE), lambda b, i: (b, 0, i, 0))
    cols = lambda n: pl.BlockSpec((1, n, LANE, tm), lambda b, i: (b, 0, 0, i))
    rows_shape = lambda n: jax.ShapeDtypeStruct((B, n, S, LANE), BF16)
    cols_shape = lambda n: jax.ShapeDtypeStruct((B, n, LANE, S), BF16)
    return pl.pallas_call(
        functools.partial(_in_proj_kernel, sub=min(tm, 256)),
        grid=(B, S // tm),
        in_specs=[
            pl.BlockSpec((1, tm, D_MODEL), lambda b, i: (b, i, 0)),
            pl.BlockSpec((4, tm, LANE), lambda b, i: (0, i, 0)),
            _const_spec((1, D_MODEL)),
            _const_spec((D_MODEL, n1)),
            _const_spec((1, LANE)), _const_spec((1, LANE)),
            _const_spec((1, Q_LORA)), _const_spec((1, KV_LORA)),
            _const_spec((Q_LORA, H_B * LANE)),
            _const_spec((KV_LORA, 2 * H_B * LANE)),
        ],
        out_specs=[rows(H_A), rows(KV_A), cols(KV_A), rows(H_B), rows(H_B), cols(H_B)],
        out_shape=[rows_shape(H_A), rows_shape(KV_A), cols_shape(KV_A),
                   rows_shape(H_B), rows_shape(H_B), cols_shape(H_B)],
        compiler_params=_params(2),
        name="in_proj",
    )(x, tabs, g_pre, w1, gqa, gka, gcq, gckv, wuq, wukv)


def _attn_kernel(q_ref, k_ref, vt_ref, o_ref, *scratch, n_heads, group, n_chunks, lead, col_split,
                 row_split):
    kc = pl.program_id(2)

    if n_chunks > 1:
        m_ref, acc_ref = scratch

        @pl.when(kc == 0)
        def _():
            m_ref[...] = jnp.full(m_ref.shape, -jnp.inf, F32)
            acc_ref[...] = jnp.zeros(acc_ref.shape, F32)

    tq = q_ref.shape[2]
    width = tq // col_split
    n_units = n_heads * col_split

    def cols(u):
        return slice((u % col_split) * width, (u % col_split + 1) * width)

    tk = k_ref.shape[2]
    rows = tk // row_split

    def score_chunk(u, r):
        h = u // col_split
        return lax.dot_general(k_ref[0, h // group, r * rows:(r + 1) * rows], q_ref[0, h, cols(u)],
                               (((1,), (1,)), ((), ())), preferred_element_type=F32)

    accs = [None] * n_units
    scores = {u: [score_chunk(u, r) for r in range(row_split)] for u in range(min(lead, n_units))}
    for u in range(n_units):
        h = u // col_split
        s = scores.pop(u)
        m = jnp.max(functools.reduce(jnp.maximum, s), axis=0, keepdims=True)
        if n_chunks > 1:
            m_prev = m_ref[h, :, cols(u)]
            m_new = jnp.maximum(m_prev, m)
            alpha = jnp.exp2(m_prev - m_new)[:1]
            m_ref[h, :, cols(u)] = m_new
            m = m_new[:1]
        nxt = []
        acc = None
        for r in range(row_split):
            if u + lead < n_units:
                nxt.append(score_chunk(u + lead, r))
            p = jnp.exp2(s[r] - m).astype(BF16)
            pv = _dot(vt_ref[0, h // group, :, r * rows:(r + 1) * rows], p)
            acc = pv if acc is None else acc + pv
        if u + lead < n_units:
            scores[u + lead] = nxt
        if n_chunks > 1:
            acc = alpha * acc_ref[h, :, cols(u)] + acc
            acc_ref[h, :, cols(u)] = acc
        accs[u] = acc

    def emit():
        for c in range(n_heads // 2):
            pair = []
            for h in (2 * c, 2 * c + 1):
                if n_chunks == 1:
                    a = jnp.concatenate(accs[h * col_split:(h + 1) * col_split], axis=1)
                else:
                    a = acc_ref[h]
                pair.append(a[:HALF] * (1.0 / a[HALF:HALF + 1]))
            o_ref[0, :, c * LANE:(c + 1) * LANE] = jnp.concatenate(pair, axis=0).T.astype(o_ref.dtype)

    if n_chunks == 1:
        emit()
    else:
        pl.when(kc == n_chunks - 1)(emit)


def _attention(q, k, vt, tq, tk):
    B, n_heads, S, _ = q.shape
    n_kv = k.shape[1]
    n_chunks = S // tk
    scratch = []
    if n_chunks > 1:
        scratch = [pltpu.VMEM((n_heads, 8, tq), F32), pltpu.VMEM((n_heads, LANE, tq), F32)]
    kernel = functools.partial(_attn_kernel, n_heads=n_heads, group=n_heads // n_kv,
                               n_chunks=n_chunks, lead=2, col_split=1, row_split=8)
    return pl.pallas_call(
        kernel,
        grid=(B, S // tq, n_chunks),
        in_specs=[
            pl.BlockSpec((1, n_heads, tq, LANE), lambda b, i, c: (b, 0, i, 0)),
            pl.BlockSpec((1, n_kv, tk, LANE), lambda b, i, c: (b, 0, c, 0)),
            pl.BlockSpec((1, n_kv, LANE, tk), lambda b, i, c: (b, 0, 0, c)),
        ],
        out_specs=pl.BlockSpec((1, tq, n_heads * HALF), lambda b, i, c: (b, i, 0)),
        out_shape=jax.ShapeDtypeStruct((B, S, n_heads * HALF), BF16),
        scratch_shapes=scratch,
        compiler_params=_params(3, last_arbitrary=True),
        name=f"attention_kv{n_kv}",
    )(q, k, vt)


def _merge_kernel(x_ref, ya_ref, yb_ref, g_pre_ref, wg_ref, bg_ref, woa_ref, wob_ref, wout_ref,
                  g_post_ref, o_ref):
    x = x_ref[0]
    h = _rms(x, g_pre_ref[...]).astype(BF16)
    gates = jax.nn.sigmoid(_dot(h, wg_ref[...]) + bg_ref[...])
    pa = _dot(ya_ref[0], woa_ref[...])
    pb = _dot(yb_ref[0], wob_ref[...])
    merged = (gates[:, :D_MODEL] * pa + gates[:, D_MODEL:] * pb).astype(BF16)
    mixed = _dot(merged, wout_ref[...])
    o_ref[0] = x + _rms(mixed, g_post_ref[...])


def _merge(x, ya, yb, g_pre, wg, bg, woa, wob, wout, g_post, tm):
    B, S, _ = x.shape
    tok = lambda w: pl.BlockSpec((1, tm, w), lambda b, i: (b, i, 0))
    return pl.pallas_call(
        _merge_kernel,
        grid=(B, S // tm),
        in_specs=[
            tok(D_MODEL), tok(W_QA), tok(W_OB),
            _const_spec((1, D_MODEL)),
            _const_spec((D_MODEL, 2 * D_MODEL)), _const_spec((1, 2 * D_MODEL)),
            _const_spec((W_QA, D_MODEL)), _const_spec((W_OB, D_MODEL)),
            _const_spec((D_MODEL, D_MODEL)), _const_spec((1, D_MODEL)),
        ],
        out_specs=tok(D_MODEL),
        out_shape=jax.ShapeDtypeStruct((B, S, D_MODEL), F32),
        compiler_params=_params(2),
        name="gated_merge",
    )(x, ya, yb, g_pre, wg, bg, woa, wob, wout, g_post)


def _ffn_kernel(x_ref, xp_ref, xn_ref, g_pre_ref, wup_ref, cw_ref, cb_ref, wdown_ref, g_post_ref,
                o_ref, f_ref, *, n_chunks):
    tm = x_ref.shape[1]
    i = pl.program_id(1)
    x = x_ref[0]
    xp = jnp.where(i > 0, xp_ref[0], 0.0)
    xn = jnp.where(i < pl.num_programs(1) - 1, xn_ref[0], 0.0)
    xa = jnp.concatenate([xp, x, xn], axis=0)
    h = _rms(xa, g_pre_ref[...]).astype(BF16)
    rows = tm + 2 * HALO
    width = D_FF // n_chunks

    def conv(col0):
        u = _dot(h, wup_ref[:, col0:col0 + width])
        um = pltpu.roll(u, 1, axis=0)[HALO:HALO + tm]
        un = pltpu.roll(u, rows - 1, axis=0)[HALO:HALO + tm]
        cw = cw_ref[:, col0:col0 + width]
        return (um * cw[0:1] + u[HALO:HALO + tm] * cw[1:2] + un * cw[2:3]
                + cb_ref[:, col0:col0 + width])

    for c in range(n_chunks):
        gate = conv(c * width)
        val = conv(D_FF + c * width)
        f_ref[:, c * width:(c + 1) * width] = (jax.nn.gelu(gate, approximate=True) * val).astype(BF16)

    out = _dot(f_ref[...], wdown_ref[...])
    o_ref[0] = x + _rms(out, g_post_ref[...])


def _ffn(x, g_pre, wup, cw, cb, wdown, g_post, tm, n_chunks):
    B, S, _ = x.shape
    per = tm // HALO
    last = S // HALO - 1
    tok = pl.BlockSpec((1, tm, D_MODEL), lambda b, i: (b, i, 0))
    kernel = functools.partial(_ffn_kernel, n_chunks=n_chunks)
    return pl.pallas_call(
        kernel,
        grid=(B, S // tm),
        in_specs=[
            tok,
            pl.BlockSpec((1, HALO, D_MODEL), lambda b, i: (b, jnp.maximum(i * per - 1, 0), 0)),
            pl.BlockSpec((1, HALO, D_MODEL), lambda b, i: (b, jnp.minimum((i + 1) * per, last), 0)),
            _const_spec((1, D_MODEL)),
            _const_spec((D_MODEL, 2 * D_FF)),
            _const_spec((CONV_W, 2 * D_FF)), _const_spec((1, 2 * D_FF)),
            _const_spec((D_FF, D_MODEL)), _const_spec((1, D_MODEL)),
        ],
        out_specs=tok,
        out_shape=jax.ShapeDtypeStruct((B, S, D_MODEL), F32),
        scratch_shapes=[pltpu.VMEM((tm, D_FF), BF16)],
        compiler_params=_params(2),
        name="channel_mixer",
    )(x, x, x, g_pre, wup, cw, cb, wdown, g_post)


def _prepare_weights(g_mix_pre, w_in, g_qa, g_ka, g_cq, w_uq, g_ckv, w_ukv, w_oa, w_ob, b_gates,
                     w_out, g_mix_post, g_ffn_pre, w_up, conv_w, conv_b, w_down, g_ffn_post):
    cols1, gate0 = _in_proj_cols()
    rot = _head_cols_rot(0, HD_A)
    row = lambda a: a[:, None, :]
    return dict(
        g_pre=row(g_mix_pre),
        w1=_gather_cols(w_in, cols1).astype(BF16),
        wg=w_in[:, :, gate0:].astype(BF16),
        gqa=row(_gather_cols(g_qa, rot)) * (HD_A ** -0.5 * LOG2E),
        gka=row(_gather_cols(g_ka, rot)),
        gcq=row(g_cq), gckv=row(g_ckv),
        wuq=_gather_cols(w_uq, _uq_cols()).astype(BF16),
        wukv=_gather_cols(w_ukv, _ukv_cols()).astype(BF16),
        woa=w_oa.astype(BF16), wob=w_ob.astype(BF16), bg=row(b_gates),
        wout=w_out.astype(BF16), g_post=row(g_mix_post),
        g_ffn_pre=row(g_ffn_pre), wup=w_up.astype(BF16), cw=conv_w, cb=row(conv_b),
        wdown=w_down.astype(BF16), g_ffn_post=row(g_ffn_post),
    )


def _tiles(seq_len):
    return dict(tm=1024, tq_a=512, tq_b=512, tk=min(seq_len, 2048))


def _trunk(x, w):
    S = x.shape[1]
    t = _tiles(S)
    tabs = _rot_tables(S)
    depth = w["w1"].shape[0]
    for l in range(depth):
        p = {k: v[l] for k, v in w.items()}
        qa, ka, va, qb, kb, vb = _in_proj(x, tabs, p["g_pre"], p["w1"], p["gqa"], p["gka"],
                                          p["gcq"], p["gckv"], p["wuq"], p["wukv"], t["tm"])
        ya = _attention(qa, ka, va, t["tq_a"], t["tk"])
        yb = _attention(qb, kb, vb, t["tq_b"], t["tk"])
        x = _merge(x, ya, yb, p["g_pre"], p["wg"], p["bg"], p["woa"], p["wob"], p["wout"],
                   p["g_post"], t["tm"])
        x = _ffn(x, p["g_ffn_pre"], p["wup"], p["cw"], p["cb"], p["wdown"], p["g_ffn_post"],
                 t["tm"], n_chunks=11)
    return x


def kernel(x_prompt, x_sample, g_mix_pre, w_in, g_qa, g_ka, g_cq, w_uq, g_ckv, w_ukv, w_oa, w_ob,
           b_gates, w_out, g_mix_post, g_ffn_pre, w_up, conv_w, conv_b, w_down, g_ffn_post):
    w = _prepare_weights(g_mix_pre, w_in, g_qa, g_ka, g_cq, w_uq, g_ckv, w_ukv, w_oa, w_ob,
                         b_gates, w_out, g_mix_post, g_ffn_pre, w_up, conv_w, conv_b, w_down,
                         g_ffn_post)
    return (_trunk(x_prompt, w), _trunk(x_sample, w))
```

```python
import functools

import jax
import jax.numpy as jnp
import numpy as np
from jax import lax
from jax.experimental import pallas as pl
from jax.experimental.pallas import tpu as pltpu

D_MODEL = 1024
GRID_W = 64
ROPE_THETA = 10000.0
EPS = 1e-6

H_A = 8
KV_A = 2
HD_A = 64
H_B = 8
Q_LORA = 384
KV_LORA = 256
NOPE_B = 64
ROPE_B = 32
V_B = 64
QK_B = NOPE_B + ROPE_B
D_FF = 2816
CONV_W = 3

W_QA = H_A * HD_A
W_KA = KV_A * HD_A
W_OB = H_B * V_B

LANE = 128
HALF = LANE // 2
HALO = 8
VMEM_LIMIT = 56 * 1024 * 1024
LOG2E = 1.4426950408889634

BF16 = jnp.bfloat16
F32 = jnp.float32


def _head_cols_rot(base, dim):
    half = dim // 2
    cols = np.full((LANE,), -1, np.int64)
    cols[:half] = base + np.arange(0, dim, 2)
    cols[HALF:HALF + half] = base + np.arange(1, dim, 2)
    return cols


def _head_cols_plain(base, dim):
    cols = np.full((LANE,), -1, np.int64)
    cols[:dim] = base + np.arange(dim)
    return cols


def _mla_cols(nope_base, rope_base):
    cols = np.full((LANE,), -1, np.int64)
    hn = NOPE_B // 2
    hr = ROPE_B // 2
    if nope_base is not None:
        cols[:hn] = nope_base + np.arange(hn)
        cols[HALF:HALF + hn] = nope_base + hn + np.arange(hn)
    if rope_base is not None:
        cols[hn:hn + hr] = rope_base + np.arange(0, ROPE_B, 2)
        cols[HALF + hn:HALF + hn + hr] = rope_base + np.arange(1, ROPE_B, 2)
    return cols


def _in_proj_cols():
    o_ka = W_QA
    o_va = o_ka + W_KA
    o_cq = o_va + W_KA
    o_ckv = o_cq + Q_LORA
    o_kr = o_ckv + KV_LORA
    blocks = [_head_cols_rot(h * HD_A, HD_A) for h in range(H_A)]
    blocks += [_head_cols_rot(o_ka + j * HD_A, HD_A) for j in range(KV_A)]
    blocks += [_head_cols_plain(o_va + j * HD_A, HD_A) for j in range(KV_A)]
    blocks += [o_cq + np.arange(Q_LORA), o_ckv + np.arange(KV_LORA)]
    blocks += [_mla_cols(None, o_kr)]
    return np.concatenate(blocks), o_kr + ROPE_B


def _uq_cols():
    return np.concatenate([_mla_cols(h * QK_B, h * QK_B + NOPE_B) for h in range(H_B)])


def _ukv_cols():
    stride = NOPE_B + V_B
    k = [_mla_cols(h * stride, None) for h in range(H_B)]
    v = [_head_cols_plain(h * stride + NOPE_B, V_B) for h in range(H_B)]
    return np.concatenate(k + v)


def _gather_cols(w, cols):
    taken = jnp.take(w, jnp.asarray(np.maximum(cols, 0), jnp.int32), axis=-1)
    return jnp.where(jnp.asarray(cols >= 0), taken, 0.0)


def _rot_tables(seq_len):
    pos = jnp.arange(seq_len, dtype=jnp.int32)
    row_id = (pos // GRID_W).astype(F32)[:, None]
    col_id = (pos % GRID_W).astype(F32)[:, None]

    def tables(rot_dim, lo):
        half = rot_dim // 2
        inv = ROPE_THETA ** (-jnp.arange(0, half, 2, dtype=F32) / half)
        freq = np.zeros((LANE,), np.int32)
        use_row = np.zeros((LANE,), bool)
        sign = np.zeros((LANE,), np.float32)
        for base, sg in ((lo, -1.0), (HALF + lo, 1.0)):
            lanes = base + np.arange(half)
            freq[lanes] = np.arange(half) % (half // 2)
            use_row[lanes] = np.arange(half) < half // 2
            sign[lanes] = sg
        ang = jnp.where(use_row, row_id, col_id) * inv[freq]
        rot = sign != 0.0
        return jnp.where(rot, jnp.cos(ang), 1.0), sign * jnp.sin(ang)

    ca, sa = tables(HD_A, 0)
    cb, sb = tables(ROPE_B, NOPE_B // 2)
    return jnp.stack([ca, sa, cb, sb])


def _rms(x, g):
    return x * lax.rsqrt(jnp.mean(x * x, axis=-1, keepdims=True) + EPS) * g


def _rope(y, cos, sin):
    return y * cos + pltpu.roll(y, HALF, axis=1) * sin


def _dot(a, b):
    return jnp.dot(a, b, preferred_element_type=F32)


def _const_spec(shape):
    zeros = (0,) * len(shape)
    return pl.BlockSpec(shape, lambda *_: zeros, pipeline_mode=pl.Buffered(1))


def _params(n_axes, last_arbitrary=False):
    sem = ["parallel"] * n_axes
    if last_arbitrary:
        sem[-1] = "arbitrary"
    return pltpu.CompilerParams(dimension_semantics=tuple(sem), vmem_limit_bytes=VMEM_LIMIT)


def _in_proj_kernel(x_ref, tab_ref, g_pre_ref, w1_ref, gqa_ref, gka_ref, gcq_ref, gckv_ref,
                    wuq_ref, wukv_ref,
                    qa_ref, ka_ref, va_ref, qb_ref, kb_ref, vb_ref, *, sub):
    low = lax.broadcasted_iota(jnp.int32, (1, LANE), 1) < HALF
    scale_b = QK_B ** -0.5 * LOG2E

    def value_t(blk):
        return jnp.where(low, blk, 1.0).T.astype(BF16)

    for r in range(x_ref.shape[1] // sub):
        rs = slice(r * sub, (r + 1) * sub)
        ca, sa, cb, sb = tab_ref[0, rs], tab_ref[1, rs], tab_ref[2, rs], tab_ref[3, rs]

        def head_norm_rope(blk, g):
            ms = jnp.sum(blk * blk, axis=-1, keepdims=True) * (1.0 / HD_A)
            return _rope(blk * lax.rsqrt(ms + EPS) * g, ca, sa)

        h = _rms(x_ref[0, rs], g_pre_ref[...]).astype(BF16)
        proj = _dot(h, w1_ref[...])

        off = 0
        for hh in range(H_A):
            qa_ref[0, hh, rs] = head_norm_rope(proj[:, off:off + LANE], gqa_ref[...]).astype(BF16)
            off += LANE
        for j in range(KV_A):
            ka_ref[0, j, rs] = head_norm_rope(proj[:, off:off + LANE], gka_ref[...]).astype(BF16)
            off += LANE
        for j in range(KV_A):
            va_ref[0, j, :, rs] = value_t(proj[:, off:off + LANE])
            off += LANE

        cq = _rms(proj[:, off:off + Q_LORA], gcq_ref[...]).astype(BF16)
        off += Q_LORA
        ckv = _rms(proj[:, off:off + KV_LORA], gckv_ref[...]).astype(BF16)
        off += KV_LORA
        kr = _rope(proj[:, off:off + LANE], cb, sb)

        qb = _dot(cq, wuq_ref[...])
        kvb = _dot(ckv, wukv_ref[...])
        for hh in range(H_B):
            sl = slice(hh * LANE, (hh + 1) * LANE)
            qb_ref[0, hh, rs] = (_rope(qb[:, sl], cb, sb) * scale_b).astype(BF16)
            kb_ref[0, hh, rs] = (kvb[:, sl] + kr).astype(BF16)
            vb_ref[0, hh, :, rs] = value_t(kvb[:, H_B * LANE + hh * LANE:H_B * LANE + (hh + 1) * LANE])


def _in_proj(x, tabs, g_pre, w1, gqa, gka, gcq, gckv, wuq, wukv, tm, sub):
    B, S, _ = x.shape
    n1 = w1.shape[1]
    rows = lambda n: pl.BlockSpec((1, n, tm, LANE), lambda b, i: (b, 0, i, 0))
    cols = lambda n: pl.BlockSpec((1, n, LANE, tm), lambda b, i: (b, 0, 0, i))
    rows_shape = lambda n: jax.ShapeDtypeStruct((B, n, S, LANE), BF16)
    cols_shape = lambda n: jax.ShapeDtypeStruct((B, n, LANE, S), BF16)
    return pl.pallas_call(
        functools.partial(_in_proj_kernel, sub=sub),
        grid=(B, S // tm),
        in_specs=[
            pl.BlockSpec((1, tm, D_MODEL), lambda b, i: (b, i, 0)),
            pl.BlockSpec((4, tm, LANE), lambda b, i: (0, i, 0)),
            _const_spec((1, D_MODEL)),
            _const_spec((D_MODEL, n1)),
            _const_spec((1, LANE)), _const_spec((1, LANE)),
            _const_spec((1, Q_LORA)), _const_spec((1, KV_LORA)),
            _const_spec((Q_LORA, H_B * LANE)),
            _const_spec((KV_LORA, 2 * H_B * LANE)),
        ],
        out_specs=[rows(H_A), rows(KV_A), cols(KV_A), rows(H_B), rows(H_B), cols(H_B)],
        out_shape=[rows_shape(H_A), rows_shape(KV_A), cols_shape(KV_A),
                   rows_shape(H_B), rows_shape(H_B), cols_shape(H_B)],
        compiler_params=_params(2),
        name="in_proj",
    )(x, tabs, g_pre, w1, gqa, gka, gcq, gckv, wuq, wukv)


def _attn_kernel(q_ref, k_ref, vt_ref, o_ref, *scratch, group, n_chunks, lead, row_chunks):
    n_heads = q_ref.shape[1]
    rows = k_ref.shape[2] // row_chunks
    kc = pl.program_id(2)

    if n_chunks > 1:
        m_ref, acc_ref = scratch

        @pl.when(kc == 0)
        def _():
            m_ref[...] = jnp.full(m_ref.shape, -jnp.inf, F32)
            acc_ref[...] = jnp.zeros(acc_ref.shape, F32)

    def score_chunk(h, r):
        return lax.dot_general(k_ref[0, h // group, r * rows:(r + 1) * rows], q_ref[0, h],
                               (((1,), (1,)), ((), ())), preferred_element_type=F32)

    accs = [None] * n_heads
    scores = {h: [score_chunk(h, r) for r in range(row_chunks)] for h in range(min(lead, n_heads))}
    for h in range(n_heads):
        s = scores.pop(h)
        m = jnp.max(functools.reduce(jnp.maximum, s), axis=0, keepdims=True)
        if n_chunks > 1:
            m_prev = m_ref[h]
            m_new = jnp.maximum(m_prev, m)
            alpha = jnp.exp2(m_prev - m_new)[:1]
            m_ref[h] = m_new
            m = m_new[:1]
        ahead = []
        acc = None
        for r in range(row_chunks):
            if h + lead < n_heads:
                ahead.append(score_chunk(h + lead, r))
            p = jnp.exp2(s[r] - m).astype(BF16)
            pv = _dot(vt_ref[0, h // group, :, r * rows:(r + 1) * rows], p)
            acc = pv if acc is None else acc + pv
        if h + lead < n_heads:
            scores[h + lead] = ahead
        if n_chunks > 1:
            acc = alpha * acc_ref[h] + acc
            acc_ref[h] = acc
        accs[h] = acc

    def emit():
        for c in range(n_heads // 2):
            pair = []
            for h in (2 * c, 2 * c + 1):
                a = accs[h] if n_chunks == 1 else acc_ref[h]
                pair.append(a[:HALF] * (1.0 / a[HALF:HALF + 1]))
            o_ref[0, :, c * LANE:(c + 1) * LANE] = jnp.concatenate(pair, axis=0).T.astype(o_ref.dtype)

    if n_chunks == 1:
        emit()
    else:
        pl.when(kc == n_chunks - 1)(emit)


def _attention(q, k, vt, tq, tk, lead, row_chunks):
    B, n_heads, S, _ = q.shape
    n_kv = k.shape[1]
    n_chunks = S // tk
    scratch = []
    if n_chunks > 1:
        scratch = [pltpu.VMEM((n_heads, 8, tq), F32), pltpu.VMEM((n_heads, LANE, tq), F32)]
    kernel = functools.partial(_attn_kernel, group=n_heads // n_kv, n_chunks=n_chunks, lead=lead,
                               row_chunks=row_chunks)
    return pl.pallas_call(
        kernel,
        grid=(B, S // tq, n_chunks),
        in_specs=[
            pl.BlockSpec((1, n_heads, tq, LANE), lambda b, i, c: (b, 0, i, 0)),
            pl.BlockSpec((1, n_kv, tk, LANE), lambda b, i, c: (b, 0, c, 0)),
            pl.BlockSpec((1, n_kv, LANE, tk), lambda b, i, c: (b, 0, 0, c)),
        ],
        out_specs=pl.BlockSpec((1, tq, n_heads * HALF), lambda b, i, c: (b, i, 0)),
        out_shape=jax.ShapeDtypeStruct((B, S, n_heads * HALF), BF16),
        scratch_shapes=scratch,
        compiler_params=_params(3, last_arbitrary=True),
        name=f"attention_kv{n_kv}",
    )(q, k, vt)


def _merge_kernel(x_ref, ya_ref, yb_ref, g_pre_ref, wg_ref, bg_ref, woa_ref, wob_ref, wout_ref,
                  g_post_ref, o_ref, *, sub):
    for r in range(x_ref.shape[1] // sub):
        rs = slice(r * sub, (r + 1) * sub)
        x = x_ref[0, rs]
        h = _rms(x, g_pre_ref[...]).astype(BF16)
        gates = jax.nn.sigmoid(_dot(h, wg_ref[...]) + bg_ref[...])
        pa = _dot(ya_ref[0, rs], woa_ref[...])
        pb = _dot(yb_ref[0, rs], wob_ref[...])
        merged = (gates[:, :D_MODEL] * pa + gates[:, D_MODEL:] * pb).astype(BF16)
        mixed = _dot(merged, wout_ref[...])
        o_ref[0, rs] = x + _rms(mixed, g_post_ref[...])


def _merge(x, ya, yb, g_pre, wg, bg, woa, wob, wout, g_post, tm, sub):
    B, S, _ = x.shape
    tok = lambda w: pl.BlockSpec((1, tm, w), lambda b, i: (b, i, 0))
    return pl.pallas_call(
        functools.partial(_merge_kernel, sub=sub),
        grid=(B, S // tm),
        in_specs=[
            tok(D_MODEL), tok(W_QA), tok(W_OB),
            _const_spec((1, D_MODEL)),
            _const_spec((D_MODEL, 2 * D_MODEL)), _const_spec((1, 2 * D_MODEL)),
            _const_spec((W_QA, D_MODEL)), _const_spec((W_OB, D_MODEL)),
            _const_spec((D_MODEL, D_MODEL)), _const_spec((1, D_MODEL)),
        ],
        out_specs=tok(D_MODEL),
        out_shape=jax.ShapeDtypeStruct((B, S, D_MODEL), F32),
        compiler_params=_params(2),
        name="gated_merge",
    )(x, ya, yb, g_pre, wg, bg, woa, wob, wout, g_post)


def _ffn_kernel(x_ref, xp_ref, xn_ref, g_pre_ref, wup_ref, cw_ref, cb_ref, wdown_ref, g_post_ref,
                o_ref, f_ref, *, n_chunks):
    tm = x_ref.shape[1]
    i = pl.program_id(1)
    x = x_ref[0]
    xp = jnp.where(i > 0, xp_ref[0], 0.0)
    xn = jnp.where(i < pl.num_programs(1) - 1, xn_ref[0], 0.0)
    xa = jnp.concatenate([xp, x, xn], axis=0)
    h = _rms(xa, g_pre_ref[...]).astype(BF16)
    rows = tm + 2 * HALO
    width = D_FF // n_chunks

    def conv(col0):
        u = _dot(h, wup_ref[:, col0:col0 + width])
        um = pltpu.roll(u, 1, axis=0)[HALO:HALO + tm]
        un = pltpu.roll(u, rows - 1, axis=0)[HALO:HALO + tm]
        cw = cw_ref[:, col0:col0 + width]
        return (um * cw[0:1] + u[HALO:HALO + tm] * cw[1:2] + un * cw[2:3]
                + cb_ref[:, col0:col0 + width])

    for c in range(n_chunks):
        gate = conv(c * width)
        val = conv(D_FF + c * width)
        f_ref[:, c * width:(c + 1) * width] = (jax.nn.gelu(gate, approximate=True) * val).astype(BF16)

    out = _dot(f_ref[...], wdown_ref[...])
    o_ref[0] = x + _rms(out, g_post_ref[...])


def _ffn(x, g_pre, wup, cw, cb, wdown, g_post, tm, n_chunks):
    B, S, _ = x.shape
    per = tm // HALO
    last = S // HALO - 1
    tok = pl.BlockSpec((1, tm, D_MODEL), lambda b, i: (b, i, 0))
    kernel = functools.partial(_ffn_kernel, n_chunks=n_chunks)
    return pl.pallas_call(
        kernel,
        grid=(B, S // tm),
        in_specs=[
            tok,
            pl.BlockSpec((1, HALO, D_MODEL), lambda b, i: (b, jnp.maximum(i * per - 1, 0), 0)),
            pl.BlockSpec((1, HALO, D_MODEL), lambda b, i: (b, jnp.minimum((i + 1) * per, last), 0)),
            _const_spec((1, D_MODEL)),
            _const_spec((D_MODEL, 2 * D_FF)),
            _const_spec((CONV_W, 2 * D_FF)), _const_spec((1, 2 * D_FF)),
            _const_spec((D_FF, D_MODEL)), _const_spec((1, D_MODEL)),
        ],
        out_specs=tok,
        out_shape=jax.ShapeDtypeStruct((B, S, D_MODEL), F32),
        scratch_shapes=[pltpu.VMEM((tm, D_FF), BF16)],
        compiler_params=_params(2),
        name="channel_mixer",
    )(x, x, x, g_pre, wup, cw, cb, wdown, g_post)


def _prepare_weights(g_mix_pre, w_in, g_qa, g_ka, g_cq, w_uq, g_ckv, w_ukv, w_oa, w_ob, b_gates,
                     w_out, g_mix_post, g_ffn_pre, w_up, conv_w, conv_b, w_down, g_ffn_post):
    cols1, gate0 = _in_proj_cols()
    rot = _head_cols_rot(0, HD_A)
    row = lambda a: a[:, None, :]
    return dict(
        g_pre=row(g_mix_pre),
        w1=_gather_cols(w_in, cols1).astype(BF16),
        wg=w_in[:, :, gate0:].astype(BF16),
        gqa=row(_gather_cols(g_qa, rot)) * (HD_A ** -0.5 * LOG2E),
        gka=row(_gather_cols(g_ka, rot)),
        gcq=row(g_cq), gckv=row(g_ckv),
        wuq=_gather_cols(w_uq, _uq_cols()).astype(BF16),
        wukv=_gather_cols(w_ukv, _ukv_cols()).astype(BF16),
        woa=w_oa.astype(BF16), wob=w_ob.astype(BF16), bg=row(b_gates),
        wout=w_out.astype(BF16), g_post=row(g_mix_post),
        g_ffn_pre=row(g_ffn_pre), wup=w_up.astype(BF16), cw=conv_w, cb=row(conv_b),
        wdown=w_down.astype(BF16), g_ffn_post=row(g_ffn_post),
    )


def _tiles(seq_len):
    tm = min(seq_len, 1024)
    return dict(tm=tm, sub=min(tm, 256), ff_chunks=11, tq=min(seq_len, 1024), tk=min(seq_len, 2048),
                lead=2, row_chunks=8)


def _trunk(x, w):
    S = x.shape[1]
    t = _tiles(S)
    tabs = _rot_tables(S)
    depth = w["w1"].shape[0]
    for l in range(depth):
        p = {k: v[l] for k, v in w.items()}
        qa, ka, va, qb, kb, vb = _in_proj(x, tabs, p["g_pre"], p["w1"], p["gqa"], p["gka"],
                                          p["gcq"], p["gckv"], p["wuq"], p["wukv"], t["tm"], t["sub"])
        ya = _attention(qa, ka, va, t["tq"], t["tk"], t["lead"], t["row_chunks"])
        yb = _attention(qb, kb, vb, t["tq"], t["tk"], t["lead"], t["row_chunks"])
        x = _merge(x, ya, yb, p["g_pre"], p["wg"], p["bg"], p["woa"], p["wob"], p["wout"],
                   p["g_post"], t["tm"], t["sub"])
        x = _ffn(x, p["g_ffn_pre"], p["wup"], p["cw"], p["cb"], p["wdown"], p["g_ffn_post"],
                 t["tm"], t["ff_chunks"])
    return x


def kernel(x_prompt, x_sample, g_mix_pre, w_in, g_qa, g_ka, g_cq, w_uq, g_ckv, w_ukv, w_oa, w_ob,
           b_gates, w_out, g_mix_post, g_ffn_pre, w_up, conv_w, conv_b, w_down, g_ffn_post):
    w = _prepare_weights(g_mix_pre, w_in, g_qa, g_ka, g_cq, w_uq, g_ckv, w_ukv, w_oa, w_ob,
                         b_gates, w_out, g_mix_post, g_ffn_pre, w_up, conv_w, conv_b, w_down,
                         g_ffn_post)
    return (_trunk(x_prompt, w), _trunk(x_sample, w))
```

```python
import functools

import jax
import jax.numpy as jnp
import numpy as np
from jax import lax
from jax.experimental import pallas as pl
from jax.experimental.pallas import tpu as pltpu

D_MODEL = 1024
GRID_W = 64
ROPE_THETA = 10000.0
EPS = 1e-6

H_A = 8
KV_A = 2
HD_A = 64
H_B = 8
Q_LORA = 384
KV_LORA = 256
NOPE_B = 64
ROPE_B = 32
V_B = 64
QK_B = NOPE_B + ROPE_B
D_FF = 2816
CONV_W = 3

W_QA = H_A * HD_A
W_KA = KV_A * HD_A
W_OB = H_B * V_B

LANE = 128
HALF = LANE // 2
HALO = 8
VMEM_LIMIT = 56 * 1024 * 1024
LOG2E = 1.4426950408889634

BF16 = jnp.bfloat16
F32 = jnp.float32


def _head_cols_rot(base, dim):
    half = dim // 2
    cols = np.full((LANE,), -1, np.int64)
    cols[:half] = base + np.arange(0, dim, 2)
    cols[HALF:HALF + half] = base + np.arange(1, dim, 2)
    return cols


def _head_cols_plain(base, dim):
    cols = np.full((LANE,), -1, np.int64)
    cols[:dim] = base + np.arange(dim)
    return cols


def _mla_cols(nope_base, rope_base):
    cols = np.full((LANE,), -1, np.int64)
    hn = NOPE_B // 2
    hr = ROPE_B // 2
    if nope_base is not None:
        cols[:hn] = nope_base + np.arange(hn)
        cols[HALF:HALF + hn] = nope_base + hn + np.arange(hn)
    if rope_base is not None:
        cols[hn:hn + hr] = rope_base + np.arange(0, ROPE_B, 2)
        cols[HALF + hn:HALF + hn + hr] = rope_base + np.arange(1, ROPE_B, 2)
    return cols


def _in_proj_cols():
    o_ka = W_QA
    o_va = o_ka + W_KA
    o_cq = o_va + W_KA
    o_ckv = o_cq + Q_LORA
    o_kr = o_ckv + KV_LORA
    blocks = [_head_cols_rot(h * HD_A, HD_A) for h in range(H_A)]
    blocks += [_head_cols_rot(o_ka + j * HD_A, HD_A) for j in range(KV_A)]
    blocks += [_head_cols_plain(o_va + j * HD_A, HD_A) for j in range(KV_A)]
    blocks += [o_cq + np.arange(Q_LORA), o_ckv + np.arange(KV_LORA)]
    blocks += [_mla_cols(None, o_kr)]
    return np.concatenate(blocks), o_kr + ROPE_B


def _uq_cols():
    return np.concatenate([_mla_cols(h * QK_B, h * QK_B + NOPE_B) for h in range(H_B)])


def _ukv_cols():
    stride = NOPE_B + V_B
    k = [_mla_cols(h * stride, None) for h in range(H_B)]
    v = [_head_cols_plain(h * stride + NOPE_B, V_B) for h in range(H_B)]
    return np.concatenate(k + v)


def _gather_cols(w, cols):
    taken = jnp.take(w, jnp.asarray(np.maximum(cols, 0), jnp.int32), axis=-1)
    return jnp.where(jnp.asarray(cols >= 0), taken, 0.0)


def _rot_tables(seq_len):
    pos = jnp.arange(seq_len, dtype=jnp.int32)
    row_id = (pos // GRID_W).astype(F32)[:, None]
    col_id = (pos % GRID_W).astype(F32)[:, None]

    def tables(rot_dim, lo):
        half = rot_dim // 2
        inv = ROPE_THETA ** (-jnp.arange(0, half, 2, dtype=F32) / half)
        freq = np.zeros((LANE,), np.int32)
        use_row = np.zeros((LANE,), bool)
        sign = np.zeros((LANE,), np.float32)
        for base, sg in ((lo, -1.0), (HALF + lo, 1.0)):
            lanes = base + np.arange(half)
            freq[lanes] = np.arange(half) % (half // 2)
            use_row[lanes] = np.arange(half) < half // 2
            sign[lanes] = sg
        ang = jnp.where(use_row, row_id, col_id) * inv[freq]
        rot = sign != 0.0
        return jnp.where(rot, jnp.cos(ang), 1.0), sign * jnp.sin(ang)

    ca, sa = tables(HD_A, 0)
    cb, sb = tables(ROPE_B, NOPE_B // 2)
    return jnp.stack([ca, sa, cb, sb])


def _rms(x, g):
    return x * lax.rsqrt(jnp.mean(x * x, axis=-1, keepdims=True) + EPS) * g


def _rope(y, cos, sin):
    return y * cos + pltpu.roll(y, HALF, axis=1) * sin


def _dot(a, b):
    return jnp.dot(a, b, preferred_element_type=F32)


def _const_spec(shape):
    zeros = (0,) * len(shape)
    return pl.BlockSpec(shape, lambda *_: zeros, pipeline_mode=pl.Buffered(1))


def _params(n_axes, last_arbitrary=False):
    sem = ["parallel"] * n_axes
    if last_arbitrary:
        sem[-1] = "arbitrary"
    return pltpu.CompilerParams(dimension_semantics=tuple(sem), vmem_limit_bytes=VMEM_LIMIT)


def _in_proj_kernel(x_ref, tab_ref, g_pre_ref, w1_ref, gqa_ref, gka_ref, gcq_ref, gckv_ref,
                    wuq_ref, wukv_ref,
                    qa_ref, ka_ref, va_ref, qb_ref, kb_ref, vb_ref, *, sub):
    low = lax.broadcasted_iota(jnp.int32, (1, LANE), 1) < HALF
    scale_b = QK_B ** -0.5 * LOG2E

    def value_t(blk):
        return jnp.where(low, blk, 1.0).T.astype(BF16)

    for r in range(x_ref.shape[1] // sub):
        rs = slice(r * sub, (r + 1) * sub)
        ca, sa, cb, sb = tab_ref[0, rs], tab_ref[1, rs], tab_ref[2, rs], tab_ref[3, rs]

        def head_norm_rope(blk, g):
            ms = jnp.sum(blk * blk, axis=-1, keepdims=True) * (1.0 / HD_A)
            return _rope(blk * lax.rsqrt(ms + EPS) * g, ca, sa)

        h = _rms(x_ref[0, rs], g_pre_ref[...]).astype(BF16)
        proj = _dot(h, w1_ref[...])

        off = 0
        for hh in range(H_A):
            qa_ref[0, hh, rs] = head_norm_rope(proj[:, off:off + LANE], gqa_ref[...]).astype(BF16)
            off += LANE
        for j in range(KV_A):
            ka_ref[0, j, rs] = head_norm_rope(proj[:, off:off + LANE], gka_ref[...]).astype(BF16)
            off += LANE
        for j in range(KV_A):
            va_ref[0, j, :, rs] = value_t(proj[:, off:off + LANE])
            off += LANE

        cq = _rms(proj[:, off:off + Q_LORA], gcq_ref[...]).astype(BF16)
        off += Q_LORA
        ckv = _rms(proj[:, off:off + KV_LORA], gckv_ref[...]).astype(BF16)
        off += KV_LORA
        kr = _rope(proj[:, off:off + LANE], cb, sb)

        qb = _dot(cq, wuq_ref[...])
        kvb = _dot(ckv, wukv_ref[...])
        for hh in range(H_B):
            sl = slice(hh * LANE, (hh + 1) * LANE)
            qb_ref[0, hh, rs] = (_rope(qb[:, sl], cb, sb) * scale_b).astype(BF16)
            kb_ref[0, hh, rs] = (kvb[:, sl] + kr).astype(BF16)
            vb_ref[0, hh, :, rs] = value_t(kvb[:, H_B * LANE + hh * LANE:H_B * LANE + (hh + 1) * LANE])


def _in_proj(x, tabs, g_pre, w1, gqa, gka, gcq, gckv, wuq, wukv, tm, sub):
    B, S, _ = x.shape
    n1 = w1.shape[1]
    rows = lambda n: pl.BlockSpec((1, n, tm, LANE), lambda b, i: (b, 0, i, 0))
    cols = lambda n: pl.BlockSpec((1, n, LANE, tm), lambda b, i: (b, 0, 0, i))
    rows_shape = lambda n: jax.ShapeDtypeStruct((B, n, S, LANE), BF16)
    cols_shape = lambda n: jax.ShapeDtypeStruct((B, n, LANE, S), BF16)
    return pl.pallas_call(
        functools.partial(_in_proj_kernel, sub=sub),
        grid=(B, S // tm),
        in_specs=[
            pl.BlockSpec((1, tm, D_MODEL), lambda b, i: (b, i, 0)),
            pl.BlockSpec((4, tm, LANE), lambda b, i: (0, i, 0)),
            _const_spec((1, D_MODEL)),
            _const_spec((D_MODEL, n1)),
            _const_spec((1, LANE)), _const_spec((1, LANE)),
            _const_spec((1, Q_LORA)), _const_spec((1, KV_LORA)),
            _const_spec((Q_LORA, H_B * LANE)),
            _const_spec((KV_LORA, 2 * H_B * LANE)),
        ],
        out_specs=[rows(H_A), rows(KV_A), cols(KV_A), rows(H_B), rows(H_B), cols(H_B)],
        out_shape=[rows_shape(H_A), rows_shape(KV_A), cols_shape(KV_A),
                   rows_shape(H_B), rows_shape(H_B), cols_shape(H_B)],
        compiler_params=_params(2),
        name="in_proj",
    )(x, tabs, g_pre, w1, gqa, gka, gcq, gckv, wuq, wukv)


def _attn_kernel(q_ref, k_ref, vt_ref, o_ref, *scratch, group, n_chunks, lead, row_chunks):
    n_heads = q_ref.shape[1]
    rows = k_ref.shape[2] // row_chunks
    kc = pl.program_id(2)

    if n_chunks > 1:
        m_ref, acc_ref = scratch

        @pl.when(kc == 0)
        def _():
            m_ref[...] = jnp.full(m_ref.shape, -jnp.inf, F32)
            acc_ref[...] = jnp.zeros(acc_ref.shape, F32)

    def score_chunk(h, r):
        return lax.dot_general(k_ref[0, h // group, r * rows:(r + 1) * rows], q_ref[0, h],
                               (((1,), (1,)), ((), ())), preferred_element_type=F32)

    accs = [None] * n_heads
    scores = {h: [score_chunk(h, r) for r in range(row_chunks)] for h in range(min(lead, n_heads))}
    for h in range(n_heads):
        s = scores.pop(h)
        m = jnp.max(functools.reduce(jnp.maximum, s), axis=0, keepdims=True)
        if n_chunks > 1:
            m_prev = m_ref[h]
            m_new = jnp.maximum(m_prev, m)
            alpha = jnp.exp2(m_prev - m_new)[:1]
            m_ref[h] = m_new
            m = m_new[:1]
        ahead = []
        acc = None
        for r in range(row_chunks):
            if h + lead < n_heads:
                ahead.append(score_chunk(h + lead, r))
            p = jnp.exp2(s[r] - m).astype(BF16)
            pv = _dot(vt_ref[0, h // group, :, r * rows:(r + 1) * rows], p)
            acc = pv if acc is None else acc + pv
        if h + lead < n_heads:
            scores[h + lead] = ahead
        if n_chunks > 1:
            acc = alpha * acc_ref[h] + acc
            acc_ref[h] = acc
        accs[h] = acc

    def emit():
        for c in range(n_heads // 2):
            pair = []
            for h in (2 * c, 2 * c + 1):
                a = accs[h] if n_chunks == 1 else acc_ref[h]
                pair.append(a[:HALF] * (1.0 / a[HALF:HALF + 1]))
            o_ref[0, :, c * LANE:(c + 1) * LANE] = jnp.concatenate(pair, axis=0).T.astype(o_ref.dtype)

    if n_chunks == 1:
        emit()
    else:
        pl.when(kc == n_chunks - 1)(emit)


def _attention(q, k, vt, tq, tk, lead, row_chunks):
    B, n_heads, S, _ = q.shape
    n_kv = k.shape[1]
    n_chunks = S // tk
    scratch = []
    if n_chunks > 1:
        scratch = [pltpu.VMEM((n_heads, 8, tq), F32), pltpu.VMEM((n_heads, LANE, tq), F32)]
    kernel = functools.partial(_attn_kernel, group=n_heads // n_kv, n_chunks=n_chunks, lead=lead,
                               row_chunks=row_chunks)
    return pl.pallas_call(
        kernel,
        grid=(B, S // tq, n_chunks),
        in_specs=[
            pl.BlockSpec((1, n_heads, tq, LANE), lambda b, i, c: (b, 0, i, 0)),
            pl.BlockSpec((1, n_kv, tk, LANE), lambda b, i, c: (b, 0, c, 0)),
            pl.BlockSpec((1, n_kv, LANE, tk), lambda b, i, c: (b, 0, 0, c)),
        ],
        out_specs=pl.BlockSpec((1, tq, n_heads * HALF), lambda b, i, c: (b, i, 0)),
        out_shape=jax.ShapeDtypeStruct((B, S, n_heads * HALF), BF16),
        scratch_shapes=scratch,
        compiler_params=_params(3, last_arbitrary=True),
        name=f"attention_kv{n_kv}",
    )(q, k, vt)


def _merge_kernel(x_ref, ya_ref, yb_ref, g_pre_ref, wg_ref, bg_ref, woa_ref, wob_ref, wout_ref,
                  g_post_ref, o_ref, *, sub):
    for r in range(x_ref.shape[1] // sub):
        rs = slice(r * sub, (r + 1) * sub)
        x = x_ref[0, rs]
        h = _rms(x, g_pre_ref[...]).astype(BF16)
        gates = jax.nn.sigmoid(_dot(h, wg_ref[...]) + bg_ref[...])
        pa = _dot(ya_ref[0, rs], woa_ref[...])
        pb = _dot(yb_ref[0, rs], wob_ref[...])
        merged = (gates[:, :D_MODEL] * pa + gates[:, D_MODEL:] * pb).astype(BF16)
        mixed = _dot(merged, wout_ref[...])
        o_ref[0, rs] = x + _rms(mixed, g_post_ref[...])


def _merge(x, ya, yb, g_pre, wg, bg, woa, wob, wout, g_post, tm, sub):
    B, S, _ = x.shape
    tok = lambda w: pl.BlockSpec((1, tm, w), lambda b, i: (b, i, 0))
    return pl.pallas_call(
        functools.partial(_merge_kernel, sub=sub),
        grid=(B, S // tm),
        in_specs=[
            tok(D_MODEL), tok(W_QA), tok(W_OB),
            _const_spec((1, D_MODEL)),
            _const_spec((D_MODEL, 2 * D_MODEL)), _const_spec((1, 2 * D_MODEL)),
            _const_spec((W_QA, D_MODEL)), _const_spec((W_OB, D_MODEL)),
            _const_spec((D_MODEL, D_MODEL)), _const_spec((1, D_MODEL)),
        ],
        out_specs=tok(D_MODEL),
        out_shape=jax.ShapeDtypeStruct((B, S, D_MODEL), F32),
        compiler_params=_params(2),
        name="gated_merge",
    )(x, ya, yb, g_pre, wg, bg, woa, wob, wout, g_post)


def _ffn_kernel(x_ref, xp_ref, xn_ref, g_pre_ref, wup_ref, cw_ref, cb_ref, wdown_ref, g_post_ref,
                o_ref, f_ref, *, n_chunks):
    tm = x_ref.shape[1]
    i = pl.program_id(1)
    x = x_ref[0]
    xp = jnp.where(i > 0, xp_ref[0], 0.0)
    xn = jnp.where(i < pl.num_programs(1) - 1, xn_ref[0], 0.0)
    xa = jnp.concatenate([xp, x, xn], axis=0)
    h = _rms(xa, g_pre_ref[...]).astype(BF16)
    rows = tm + 2 * HALO
    width = D_FF // n_chunks

    def conv(col0):
        u = _dot(h, wup_ref[:, col0:col0 + width])
        um = pltpu.roll(u, 1, axis=0)[HALO:HALO + tm]
        un = pltpu.roll(u, rows - 1, axis=0)[HALO:HALO + tm]
        cw = cw_ref[:, col0:col0 + width]
        return (um * cw[0:1] + u[HALO:HALO + tm] * cw[1:2] + un * cw[2:3]
                + cb_ref[:, col0:col0 + width])

    for c in range(n_chunks):
        gate = conv(c * width)
        val = conv(D_FF + c * width)
        f_ref[:, c * width:(c + 1) * width] = (jax.nn.gelu(gate, approximate=True) * val).astype(BF16)

    out = _dot(f_ref[...], wdown_ref[...])
    o_ref[0] = x + _rms(out, g_post_ref[...])


def _ffn(x, g_pre, wup, cw, cb, wdown, g_post, tm, n_chunks):
    B, S, _ = x.shape
    per = tm // HALO
    last = S // HALO - 1
    tok = pl.BlockSpec((1, tm, D_MODEL), lambda b, i: (b, i, 0))
    kernel = functools.partial(_ffn_kernel, n_chunks=n_chunks)
    return pl.pallas_call(
        kernel,
        grid=(B, S // tm),
        in_specs=[
            tok,
            pl.BlockSpec((1, HALO, D_MODEL), lambda b, i: (b, jnp.maximum(i * per - 1, 0), 0)),
            pl.BlockSpec((1, HALO, D_MODEL), lambda b, i: (b, jnp.minimum((i + 1) * per, last), 0)),
            _const_spec((1, D_MODEL)),
            _const_spec((D_MODEL, 2 * D_FF)),
            _const_spec((CONV_W, 2 * D_FF)), _const_spec((1, 2 * D_FF)),
            _const_spec((D_FF, D_MODEL)), _const_spec((1, D_MODEL)),
        ],
        out_specs=tok,
        out_shape=jax.ShapeDtypeStruct((B, S, D_MODEL), F32),
        scratch_shapes=[pltpu.VMEM((tm, D_FF), BF16)],
        compiler_params=_params(2),
        name="channel_mixer",
    )(x, x, x, g_pre, wup, cw, cb, wdown, g_post)


def _prepare_weights(g_mix_pre, w_in, g_qa, g_ka, g_cq, w_uq, g_ckv, w_ukv, w_oa, w_ob, b_gates,
                     w_out, g_mix_post, g_ffn_pre, w_up, conv_w, conv_b, w_down, g_ffn_post):
    cols1, gate0 = _in_proj_cols()
    rot = _head_cols_rot(0, HD_A)
    row = lambda a: a[:, None, :]
    return dict(
        g_pre=row(g_mix_pre),
        w1=_gather_cols(w_in, cols1).astype(BF16),
        wg=w_in[:, :, gate0:].astype(BF16),
        gqa=row(_gather_cols(g_qa, rot)) * (HD_A ** -0.5 * LOG2E),
        gka=row(_gather_cols(g_ka, rot)),
        gcq=row(g_cq), gckv=row(g_ckv),
        wuq=_gather_cols(w_uq, _uq_cols()).astype(BF16),
        wukv=_gather_cols(w_ukv, _ukv_cols()).astype(BF16),
        woa=w_oa.astype(BF16), wob=w_ob.astype(BF16), bg=row(b_gates),
        wout=w_out.astype(BF16), g_post=row(g_mix_post),
        g_ffn_pre=row(g_ffn_pre), wup=w_up.astype(BF16), cw=conv_w, cb=row(conv_b),
        wdown=w_down.astype(BF16), g_ffn_post=row(g_ffn_post),
    )


def _tiles(seq_len):
    tm = min(seq_len, 1024)
    return dict(tm=tm, sub=min(tm, 256), merge_sub=tm, ff_chunks=11, tq=min(seq_len, 1024),
                tk=min(seq_len, 2048), lead=2, row_chunks=4)


def _trunk(x, w):
    S = x.shape[1]
    t = _tiles(S)
    tabs = _rot_tables(S)
    depth = w["w1"].shape[0]
    for l in range(depth):
        p = {k: v[l] for k, v in w.items()}
        qa, ka, va, qb, kb, vb = _in_proj(x, tabs, p["g_pre"], p["w1"], p["gqa"], p["gka"],
                                          p["gcq"], p["gckv"], p["wuq"], p["wukv"], t["tm"], t["sub"])
        ya = _attention(qa, ka, va, t["tq"], t["tk"], t["lead"], t["row_chunks"])
        yb = _attention(qb, kb, vb, t["tq"], t["tk"], t["lead"], t["row_chunks"])
        x = _merge(x, ya, yb, p["g_pre"], p["wg"], p["bg"], p["woa"], p["wob"], p["wout"],
                   p["g_post"], t["tm"], t["merge_sub"])
        x = _ffn(x, p["g_ffn_pre"], p["wup"], p["cw"], p["cb"], p["wdown"], p["g_ffn_post"],
                 t["tm"], t["ff_chunks"])
    return x


def kernel(x_prompt, x_sample, g_mix_pre, w_in, g_qa, g_ka, g_cq, w_uq, g_ckv, w_ukv, w_oa, w_ob,
           b_gates, w_out, g_mix_post, g_ffn_pre, w_up, conv_w, conv_b, w_down, g_ffn_post):
    w = _prepare_weights(g_mix_pre, w_in, g_qa, g_ka, g_cq, w_uq, g_ckv, w_ukv, w_oa, w_ob,
                         b_gates, w_out, g_mix_post, g_ffn_pre, w_up, conv_w, conv_b, w_down,
                         g_ffn_post)
    return (_trunk(x_prompt, w), _trunk(x_sample, w))
```
